```python
import math
import jax
import jax.numpy as jnp
from jax import lax
import numpy as np

D_MODEL = 2048
BATCH = 2
SEQ = 8192
DEPTH = 4

GDN_HEADS = 8
GDN_HEAD_DIM = 128
GDN_WIDTH = GDN_HEADS * GDN_HEAD_DIM
S5_GROUP_SIZE = 16
S5_GROUPS = 48
S5_STATE = 64
S5_WIDTH = S5_GROUPS * S5_GROUP_SIZE
M2_HEADS = 16
M2_HEAD_DIM = 64
M2_WIDTH = M2_HEADS * M2_HEAD_DIM
M2_GROUPS = 4
M2_STATE = 128
M2_CONV_DIM = M2_WIDTH + 2 * M2_GROUPS * M2_STATE
CONV_K = 4
CHUNK = 64
N_BRANCHES = 3
NORM_EPS = 1e-6
IN_SPLITS = (3 * GDN_WIDTH, GDN_WIDTH, GDN_HEADS, GDN_HEADS,
             S5_WIDTH, S5_WIDTH,
             M2_WIDTH, M2_CONV_DIM, M2_HEADS,
             N_BRANCHES * D_MODEL)
IN_DIM = 4 * GDN_WIDTH + 2 * GDN_HEADS + 2 * S5_WIDTH + M2_WIDTH + M2_CONV_DIM + M2_HEADS + N_BRANCHES * D_MODEL

kernel_name = 'hybrid_gdn_s5_ssd_gated_merge'


def rms_norm(x, w):
    x32 = x.astype(jnp.float32)
    y = x32 * lax.rsqrt(jnp.mean(x32 * x32, axis=-1, keepdims=True) + NORM_EPS)
    return (y * w.astype(jnp.float32)).astype(x.dtype)


def _rms_f32(x, w):
    return x * lax.rsqrt(jnp.mean(x * x, axis=-1, keepdims=True) + NORM_EPS) * w


def _l2norm(x):
    return x * lax.rsqrt(jnp.sum(x * x, axis=-1, keepdims=True) + NORM_EPS)


def causal_depthwise_conv(x, w, b=None):
    k, c = w.shape
    y = lax.conv_general_dilated(x, w[:, None, :].astype(x.dtype), window_strides=(1,),
                                 padding=[(k - 1, 0)], dimension_numbers=('NWC', 'WIO', 'NWC'),
                                 feature_group_count=c)
    if b is not None:
        y = y + b.astype(x.dtype)
    return y


def gated_delta_rule_chunked(q, k, v, g, beta):
    bsz, seqlen, heads, dk = q.shape
    dv = v.shape[-1]
    n = seqlen // CHUNK

    def to_chunks(t):
        return t.reshape(bsz, n, CHUNK, heads, -1).transpose(0, 1, 3, 2, 4)

    q = to_chunks(q) * (dk ** -0.5)
    k = to_chunks(k)
    v = to_chunks(v)
    beta = beta.reshape(bsz, n, CHUNK, heads).transpose(0, 1, 3, 2)
    g_cum = jnp.cumsum(g.reshape(bsz, n, CHUNK, heads).transpose(0, 1, 3, 2), axis=-1)
    causal = jnp.tril(jnp.ones((CHUNK, CHUNK), dtype=bool))
    strict = jnp.tril(jnp.ones((CHUNK, CHUNK), dtype=bool), k=-1)
    decay = jnp.exp(jnp.where(causal, g_cum[..., :, None] - g_cum[..., None, :], -jnp.inf))
    k_beta = k * beta[..., None]
    a_mat = jnp.where(strict, jnp.einsum('bnhid,bnhjd->bnhij', k_beta, k) * decay, 0.0)
    eye = jnp.eye(CHUNK, dtype=q.dtype)
    t_inv = lax.linalg.triangular_solve(a_mat + eye, jnp.broadcast_to(eye, a_mat.shape),
                                        left_side=True, lower=True)
    u = t_inv @ (v * beta[..., None])
    w = t_inv @ (k_beta * jnp.exp(g_cum)[..., None])
    qk = jnp.einsum('bnhid,bnhjd->bnhij', q, k) * decay
    q_dec = q * jnp.exp(g_cum)[..., None]
    g_last = g_cum[..., -1]
    k_tail = k * jnp.exp(g_last[..., None] - g_cum)[..., None]

    def step(state, inp):
        q_i, qk_i, u_i, w_i, kt_i, gl_i = inp
        v_new = u_i - jnp.einsum('bhck,bhkv->bhcv', w_i, state)
        out = jnp.einsum('bhck,bhkv->bhcv', q_i, state) + jnp.einsum('bhij,bhjv->bhiv', qk_i, v_new)
        state = state * jnp.exp(gl_i)[..., None, None] + jnp.einsum('bhck,bhcv->bhkv', kt_i, v_new)
        return state, out

    init = jnp.zeros((bsz, heads, dk, dv), q.dtype)
    xs = tuple(jnp.moveaxis(t, 1, 0) for t in (q_dec, qk, u, w, k_tail, g_last))
    _, out = lax.scan(step, init, xs)
    return out.transpose(1, 0, 3, 2, 4).reshape(bsz, seqlen, heads, dv)


def gdn_branch(qkv, z, b_raw, a_raw, conv_w, a_log, dt_bias, norm_w):
    bsz, seqlen, _ = qkv.shape
    f32 = jnp.float32
    qkv = jax.nn.silu(causal_depthwise_conv(qkv.astype(f32), conv_w.astype(f32)))
    q, k, v = jnp.split(qkv, 3, axis=-1)
    shp = (bsz, seqlen, GDN_HEADS, GDN_HEAD_DIM)
    q = _l2norm(q.reshape(shp))
    k = _l2norm(k.reshape(shp))
    v = v.reshape(shp)
    beta = jax.nn.sigmoid(b_raw.astype(f32))
    g = -jnp.exp(a_log.astype(f32)) * jax.nn.softplus(a_raw.astype(f32) + dt_bias.astype(f32))
    o = gated_delta_rule_chunked(q, k, v, g, beta)
    o = _rms_f32(o, norm_w.astype(f32)) * jax.nn.silu(z.astype(f32).reshape(shp))
    return o.reshape(bsz, seqlen, GDN_WIDTH)


def _complex_affine_combine(left, right):
    a1r, a1i, b1r, b1i = left
    a2r, a2i, b2r, b2i = right
    return (a2r * a1r - a2i * a1i,
            a2r * a1i + a2i * a1r,
            a2r * b1r - a2i * b1i + b2r,
            a2r * b1i + a2i * b1r + b2i)


def s5_branch(u, gate, lam_re, lam_im, log_step, b_re, b_im, c_re, c_im, d_skip, glu_w, glu_b):
    bsz, seqlen, _ = u.shape
    f32 = jnp.float32
    u = u.astype(f32).reshape(bsz, seqlen, S5_GROUPS, S5_GROUP_SIZE)
    lam_re = jnp.minimum(lam_re.astype(f32), -1e-4)
    lam_im = lam_im.astype(f32)
    step = jnp.exp(log_step.astype(f32))[:, None]
    mag = jnp.exp(lam_re * step)
    ab_re = mag * jnp.cos(lam_im * step)
    ab_im = mag * jnp.sin(lam_im * step)
    den = lam_re * lam_re + lam_im * lam_im
    f_re = ((ab_re - 1.0) * lam_re + ab_im * lam_im) / den
    f_im = (ab_im * lam_re - (ab_re - 1.0) * lam_im) / den
    b_re = b_re.astype(f32)
    b_im = b_im.astype(f32)
    bb_re = f_re[..., None] * b_re - f_im[..., None] * b_im
    bb_im = f_re[..., None] * b_im + f_im[..., None] * b_re
    bu_re = jnp.einsum('blgh,gph->blgp', u, bb_re)
    bu_im = jnp.einsum('blgh,gph->blgp', u, bb_im)
    a_re = jnp.broadcast_to(ab_re, (seqlen,) + ab_re.shape)
    a_im = jnp.broadcast_to(ab_im, (seqlen,) + ab_im.shape)

    def scan_one(br, bi):
        return lax.associative_scan(_complex_affine_combine, (a_re, a_im, br, bi), axis=0)[2:]

    s_re, s_im = jax.vmap(scan_one)(bu_re, bu_im)
    y = (jnp.einsum('blgp,ghp->blgh', s_re, c_re.astype(f32))
         - jnp.einsum('blgp,ghp->blgh', s_im, c_im.astype(f32))
         + d_skip.astype(f32) * u)
    y = jax.nn.gelu(y.reshape(bsz, seqlen, S5_WIDTH))
    y = y * jax.nn.sigmoid(y @ glu_w.astype(f32) + glu_b.astype(f32))
    return y * jax.nn.silu(gate.astype(f32))


def ssd_chunked(x, a, bm, cm):
    bsz, seqlen, heads, hd = x.shape
    groups, dstate = bm.shape[2], bm.shape[3]
    rep = heads // groups
    n = seqlen // CHUNK
    x = x.reshape(bsz, n, CHUNK, groups, rep, hd)
    bm = bm.reshape(bsz, n, CHUNK, groups, dstate)
    cm = cm.reshape(bsz, n, CHUNK, groups, dstate)
    a_cum = jnp.cumsum(a.reshape(bsz, n, CHUNK, groups, rep).transpose(0, 1, 3, 4, 2), axis=-1)
    causal = jnp.tril(jnp.ones((CHUNK, CHUNK), dtype=bool))
    seg = jnp.exp(jnp.where(causal, a_cum[..., :, None] - a_cum[..., None, :], -jnp.inf))
    scores = jnp.einsum('bclgn,bcsgn->bcgls', cm, bm)
    y_diag = jnp.einsum('bcgls,bcgrls,bcsgrp->bclgrp', scores, seg, x)
    decay_states = jnp.exp(a_cum[..., -1:] - a_cum)
    states = jnp.einsum('bclgn,bcgrl,bclgrp->bcgrpn', bm, decay_states, x)
    chunk_decay = jnp.exp(a_cum[..., -1])

    def step(s, inp):
        st, dec = inp
        return s * dec[..., None, None] + st, s

    init = jnp.zeros((bsz, groups, rep, hd, dstate), x.dtype)
    _, s_prev = lax.scan(step, init, (jnp.moveaxis(states, 1, 0), jnp.moveaxis(chunk_decay, 1, 0)))
    s_prev = jnp.moveaxis(s_prev, 0, 1)
    y_off = jnp.einsum('bclgn,bcgrpn,bcgrl->bclgrp', cm, s_prev, jnp.exp(a_cum))
    return (y_diag + y_off).reshape(bsz, seqlen, heads, hd)


def mamba2_branch(z, xbc, dt_raw, conv_w, conv_b, a_log, dt_bias, d_skip, norm_w):
    bsz, seqlen, _ = z.shape
    f32 = jnp.float32
    xbc = jax.nn.silu(causal_depthwise_conv(xbc.astype(f32), conv_w.astype(f32), conv_b.astype(f32)))
    xs, bm, cm = jnp.split(xbc, [M2_WIDTH, M2_WIDTH + M2_GROUPS * M2_STATE], axis=-1)
    xs = xs.reshape(bsz, seqlen, M2_HEADS, M2_HEAD_DIM)
    bm = bm.reshape(bsz, seqlen, M2_GROUPS, M2_STATE)
    cm = cm.reshape(bsz, seqlen, M2_GROUPS, M2_STATE)
    dt = jax.nn.softplus(dt_raw.astype(f32) + dt_bias.astype(f32))
    a = -jnp.exp(a_log.astype(f32))
    y = ssd_chunked(xs * dt[..., None], a * dt, bm, cm) + d_skip.astype(f32)[:, None] * xs
    y = y.reshape(bsz, seqlen, M2_WIDTH) * jax.nn.silu(z.astype(f32))
    y = _rms_f32(y.reshape(bsz, seqlen, M2_GROUPS, M2_WIDTH // M2_GROUPS), 1.0)
    return y.reshape(bsz, seqlen, M2_WIDTH) * norm_w.astype(f32)


def hybrid_layer(x, norm_w, w_in, gdn_conv_w, gdn_a_log, gdn_dt_bias, gdn_norm_w,
                 s5_lam_re, s5_lam_im, s5_log_step, s5_b_re, s5_b_im, s5_c_re, s5_c_im, s5_d,
                 s5_glu_w, s5_glu_b, m2_conv_w, m2_conv_b, m2_a_log, m2_dt_bias, m2_d, m2_norm_w,
                 proj_a, proj_b, proj_c, w_out):
    h = rms_norm(x, norm_w)
    proj = h @ w_in
    idx = [int(i) for i in np.cumsum(IN_SPLITS)[:-1]]
    (qkv, a_z, a_beta, a_decay, s_u, s_gate, c_z, c_xbc, c_dt, merge) = jnp.split(proj, idx, axis=-1)
    y_a = gdn_branch(qkv, a_z, a_beta, a_decay, gdn_conv_w, gdn_a_log, gdn_dt_bias, gdn_norm_w)
    y_b = s5_branch(s_u, s_gate, s5_lam_re, s5_lam_im, s5_log_step, s5_b_re, s5_b_im,
                    s5_c_re, s5_c_im, s5_d, s5_glu_w, s5_glu_b)
    y_c = mamba2_branch(c_z, c_xbc, c_dt, m2_conv_w, m2_conv_b, m2_a_log, m2_dt_bias, m2_d, m2_norm_w)
    g_a, g_b, g_c = jnp.split(jax.nn.sigmoid(merge), N_BRANCHES, axis=-1)
    dt = x.dtype
    merged = (g_a * (y_a.astype(dt) @ proj_a)
              + g_b * (y_b.astype(dt) @ proj_b)
              + g_c * (y_c.astype(dt) @ proj_c))
    return x + merged @ w_out


def _inv_softplus(y):
    return y + jnp.log(-jnp.expm1(-y))


def _log_uniform(key, shape, lo, hi):
    return jnp.exp(jax.random.uniform(key, shape, jnp.float32, math.log(lo), math.log(hi)))


def setup_inputs(seed: int = 0) -> dict:
    key = jax.random.key(seed)
    ks = jax.random.split(key, 32)
    f32 = jnp.float32

    def nrm(k, shape, scale):
        return scale * jax.random.normal(k, shape, f32)

    x = jax.random.normal(ks[0], (BATCH, SEQ, D_MODEL), f32)
    norm_w = 1.0 + nrm(ks[1], (DEPTH, D_MODEL), 0.02)
    w_in = nrm(ks[2], (DEPTH, D_MODEL, IN_DIM), D_MODEL ** -0.5)
    gdn_conv_w = nrm(ks[3], (DEPTH, CONV_K, 3 * GDN_WIDTH), CONV_K ** -0.5)
    gdn_a_log = jnp.log(jax.random.uniform(ks[4], (DEPTH, GDN_HEADS), f32, 1.0, 16.0))
    gdn_dt_bias = _inv_softplus(_log_uniform(ks[5], (DEPTH, GDN_HEADS), 1e-3, 1e-1))
    gdn_norm_w = 1.0 + nrm(ks[6], (DEPTH, GDN_HEAD_DIM), 0.02)
    n_idx = jnp.arange(S5_STATE, dtype=f32)
    s5_lam_re = -0.5 + nrm(ks[7], (DEPTH, S5_GROUPS, S5_STATE), 1e-3)
    s5_lam_im = math.pi * n_idx + nrm(ks[8], (DEPTH, S5_GROUPS, S5_STATE), 1e-3)
    s5_log_step = jax.random.uniform(ks[9], (DEPTH, S5_GROUPS), f32, math.log(1e-3), math.log(1e-1))
    s5_b_re = nrm(ks[10], (DEPTH, S5_GROUPS, S5_STATE, S5_GROUP_SIZE), S5_GROUP_SIZE ** -0.5)
    s5_b_im = nrm(ks[11], (DEPTH, S5_GROUPS, S5_STATE, S5_GROUP_SIZE), S5_GROUP_SIZE ** -0.5)
    s5_c_re = nrm(ks[12], (DEPTH, S5_GROUPS, S5_GROUP_SIZE, S5_STATE), 0.5)
    s5_c_im = nrm(ks[13], (DEPTH, S5_GROUPS, S5_GROUP_SIZE, S5_STATE), 0.5)
    s5_d = nrm(ks[14], (DEPTH, S5_GROUPS, S5_GROUP_SIZE), 1.0)
    s5_glu_w = nrm(ks[15], (DEPTH, S5_WIDTH, S5_WIDTH), S5_WIDTH ** -0.5)
    s5_glu_b = nrm(ks[16], (DEPTH, S5_WIDTH), 0.01)
    m2_conv_w = nrm(ks[17], (DEPTH, CONV_K, M2_CONV_DIM), CONV_K ** -0.5)
    m2_conv_b = nrm(ks[18], (DEPTH, M2_CONV_DIM), 0.01)
    m2_a_log = jnp.log(jax.random.uniform(ks[19], (DEPTH, M2_HEADS), f32, 1.0, 16.0))
    m2_dt_bias = _inv_softplus(_log_uniform(ks[20], (DEPTH, M2_HEADS), 1e-3, 1e-1))
    m2_d = 1.0 + nrm(ks[21], (DEPTH, M2_HEADS), 0.1)
    m2_norm_w = 1.0 + nrm(ks[22], (DEPTH, M2_WIDTH), 0.02)
    proj_a = nrm(ks[23], (DEPTH, GDN_WIDTH, D_MODEL), GDN_WIDTH ** -0.5)
    proj_b = nrm(ks[24], (DEPTH, S5_WIDTH, D_MODEL), S5_WIDTH ** -0.5)
    proj_c = nrm(ks[25], (DEPTH, M2_WIDTH, D_MODEL), M2_WIDTH ** -0.5)
    w_out = nrm(ks[26], (DEPTH, D_MODEL, D_MODEL), D_MODEL ** -0.5)
    final_norm_w = 1.0 + nrm(ks[27], (D_MODEL,), 0.02)
    return {'x': x, 'norm_w': norm_w, 'w_in': w_in,
            'gdn_conv_w': gdn_conv_w, 'gdn_a_log': gdn_a_log, 'gdn_dt_bias': gdn_dt_bias, 'gdn_norm_w': gdn_norm_w,
            's5_lam_re': s5_lam_re, 's5_lam_im': s5_lam_im, 's5_log_step': s5_log_step,
            's5_b_re': s5_b_re, 's5_b_im': s5_b_im, 's5_c_re': s5_c_re, 's5_c_im': s5_c_im, 's5_d': s5_d,
            's5_glu_w': s5_glu_w, 's5_glu_b': s5_glu_b,
            'm2_conv_w': m2_conv_w, 'm2_conv_b': m2_conv_b, 'm2_a_log': m2_a_log, 'm2_dt_bias': m2_dt_bias,
            'm2_d': m2_d, 'm2_norm_w': m2_norm_w,
            'proj_a': proj_a, 'proj_b': proj_b, 'proj_c': proj_c, 'w_out': w_out,
            'final_norm_w': final_norm_w}


def reference(x, norm_w, w_in, gdn_conv_w, gdn_a_log, gdn_dt_bias, gdn_norm_w,
              s5_lam_re, s5_lam_im, s5_log_step, s5_b_re, s5_b_im, s5_c_re, s5_c_im, s5_d,
              s5_glu_w, s5_glu_b, m2_conv_w, m2_conv_b, m2_a_log, m2_dt_bias, m2_d, m2_norm_w,
              proj_a, proj_b, proj_c, w_out, final_norm_w):
    for i in range(DEPTH):
        x = hybrid_layer(x, norm_w[i], w_in[i], gdn_conv_w[i], gdn_a_log[i], gdn_dt_bias[i], gdn_norm_w[i],
                         s5_lam_re[i], s5_lam_im[i], s5_log_step[i], s5_b_re[i], s5_b_im[i],
                         s5_c_re[i], s5_c_im[i], s5_d[i], s5_glu_w[i], s5_glu_b[i],
                         m2_conv_w[i], m2_conv_b[i], m2_a_log[i], m2_dt_bias[i], m2_d[i], m2_norm_w[i],
                         proj_a[i], proj_b[i], proj_c[i], w_out[i])
    return rms_norm(x, final_norm_w)
```

```python
import functools
import math

import jax
import jax.numpy as jnp
from jax import lax
from jax.experimental import pallas as pl
from jax.experimental.pallas import tpu as pltpu

F32 = jnp.float32
BF16 = jnp.bfloat16
HIGHEST = lax.Precision.HIGHEST

NORM_EPS = 1e-6
CHUNK = 64
CONV_K = 4
HALO = 8
LANES = 128

GDN_HEADS = 8
GDN_HEAD_DIM = 128
GDN_WIDTH = GDN_HEADS * GDN_HEAD_DIM
S5_GROUP_SIZE = 16
S5_GROUPS = 48
S5_STATE = 64
S5_WIDTH = S5_GROUPS * S5_GROUP_SIZE
M2_HEADS = 16
M2_HEAD_DIM = 64
M2_WIDTH = M2_HEADS * M2_HEAD_DIM
M2_GROUPS = 4
M2_STATE = 128
M2_BC = M2_GROUPS * M2_STATE
M2_GROUP_W = M2_WIDTH // M2_GROUPS

OFF_MERGE = 0
OFF_QKV = OFF_MERGE + 3 * 2048
OFF_XBC = OFF_QKV + 3 * GDN_WIDTH
OFF_AZ = OFF_XBC + M2_WIDTH + 2 * M2_BC
OFF_CZ = OFF_AZ + GDN_WIDTH
OFF_SU = OFF_CZ + M2_WIDTH
OFF_SG = OFF_SU + S5_WIDTH
MAIN_W = OFF_SG + S5_WIDTH
LANE_BETA = 0
LANE_DECAY = GDN_HEADS
LANE_DT = 2 * GDN_HEADS

VMEM_LIMIT = 56 * 1024 * 1024


def _softplus(x):
    return jnp.maximum(x, 0.0) + jnp.log(1.0 + jnp.exp(-jnp.abs(x)))


def _silu(x):
    return x * jax.nn.sigmoid(x)


def _dot(a, b):
    return jnp.dot(a, b, preferred_element_type=F32)


def _dot_nt(a, b):
    return lax.dot_general(a, b, (((1,), (1,)), ((), ())), preferred_element_type=F32)


def _dot_hi(a, b):
    return jnp.dot(a, b, preferred_element_type=F32, precision=HIGHEST)


def _tri(n, strict=False):
    r = lax.broadcasted_iota(jnp.int32, (n, n), 0)
    c = lax.broadcasted_iota(jnp.int32, (n, n), 1)
    return (r > c) if strict else (r >= c)


def _rmsnorm_kernel(x_ref, w_ref, o_ref):
    x = x_ref[...]
    y = x * lax.rsqrt(jnp.mean(x * x, axis=-1, keepdims=True) + NORM_EPS)
    o_ref[...] = (y * w_ref[...]).astype(o_ref.dtype)


def _rmsnorm(x2d, w, out_dtype, tm=512):
    t, d = x2d.shape
    tm = min(tm, t)
    return pl.pallas_call(
        _rmsnorm_kernel,
        out_shape=jax.ShapeDtypeStruct((t, d), out_dtype),
        grid=(t // tm,),
        in_specs=[pl.BlockSpec((tm, d), lambda i: (i, 0)),
                  pl.BlockSpec((1, d), lambda i: (0, 0))],
        out_specs=pl.BlockSpec((tm, d), lambda i: (i, 0)),
        compiler_params=pltpu.CompilerParams(dimension_semantics=("parallel",)),
        name="rmsnorm",
    )(x2d, w.reshape(1, d))


def _matmul_kernel(a_ref, b_ref, o_ref):
    o_ref[...] = _dot(a_ref[...], b_ref[...]).astype(o_ref.dtype)


def _matmul(a, b, tm, tn, out_dtype=F32):
    m, k = a.shape
    _, n = b.shape
    tm = min(tm, m)
    tn = min(tn, n)
    return pl.pallas_call(
        _matmul_kernel,
        out_shape=jax.ShapeDtypeStruct((m, n), out_dtype),
        grid=(m // tm, n // tn),
        in_specs=[pl.BlockSpec((tm, k), lambda i, j: (i, 0)),
                  pl.BlockSpec((k, tn), lambda i, j: (0, j))],
        out_specs=pl.BlockSpec((tm, tn), lambda i, j: (i, j)),
        compiler_params=pltpu.CompilerParams(
            dimension_semantics=("parallel", "parallel"), vmem_limit_bytes=VMEM_LIMIT),
        name="inproj",
    )(a, b)


def _fill_ext(ext_ref, prev_ref, cur_ref, first):
    prev = prev_ref[...]
    ext_ref[0:HALO, :] = jnp.where(first, jnp.zeros_like(prev), prev)
    ext_ref[HALO:HALO + CHUNK, :] = cur_ref[...]


def _conv_tile(ext_ref, w_ref, col, width):
    acc = None
    for k in range(CONV_K):
        start = HALO - (CONV_K - 1) + k
        term = w_ref[k:k + 1, col:col + width] * ext_ref[start:start + CHUNK, col:col + width]
        acc = term if acc is None else acc + term
    return acc


def _gdn_kernel(q_ref, qprev_ref, z_ref, sm_ref, convw_ref, alog_ref, dtb_ref, normw_ref,
                o_ref, state_ref, ext_ref):
    c = pl.program_id(1)
    first = c == 0

    @pl.when(first)
    def _():
        state_ref[...] = jnp.zeros_like(state_ref)

    _fill_ext(ext_ref, qprev_ref, q_ref, first)

    sm = sm_ref[...]
    beta_all = jax.nn.sigmoid(sm)
    g_all = -jnp.exp(alog_ref[...]) * _softplus(sm + dtb_ref[...])
    causal = _tri(CHUNK)
    strict = _tri(CHUNK, strict=True)
    ltri = causal.astype(F32)
    gc_all = _dot_hi(ltri, g_all)
    gc_t = gc_all.T
    eg_all = jnp.exp(gc_all)
    glast_all = gc_all[CHUNK - 1:CHUNK, :]
    etail_all = jnp.exp(glast_all - gc_all)
    eglast_all = jnp.exp(glast_all)
    eye = (lax.broadcasted_iota(jnp.int32, (CHUNK, CHUNK), 0)
           == lax.broadcasted_iota(jnp.int32, (CHUNK, CHUNK), 1)).astype(F32)
    scale = GDN_HEAD_DIM ** -0.5

    for h in range(GDN_HEADS):
        lo = h * GDN_HEAD_DIM
        q = _silu(_conv_tile(ext_ref, convw_ref, lo, GDN_HEAD_DIM))
        k = _silu(_conv_tile(ext_ref, convw_ref, GDN_WIDTH + lo, GDN_HEAD_DIM))
        v = _silu(_conv_tile(ext_ref, convw_ref, 2 * GDN_WIDTH + lo, GDN_HEAD_DIM))
        q = q * lax.rsqrt(jnp.sum(q * q, axis=-1, keepdims=True) + NORM_EPS) * scale
        k = k * lax.rsqrt(jnp.sum(k * k, axis=-1, keepdims=True) + NORM_EPS)

        beta = beta_all[:, LANE_BETA + h:LANE_BETA + h + 1]
        ld = LANE_DECAY + h
        gcol = gc_all[:, ld:ld + 1]
        grow = gc_t[ld:ld + 1, :]
        eg = eg_all[:, ld:ld + 1]
        etail = etail_all[:, ld:ld + 1]
        eglast = eglast_all[:, ld:ld + 1]
        decay = jnp.exp(jnp.where(causal, gcol - grow, -jnp.inf))

        kb = k * beta
        kk = _dot_nt(jnp.concatenate([kb, q], axis=0).astype(BF16), k.astype(BF16))
        a_mat = jnp.where(strict, kk[:CHUNK] * decay, 0.0)
        qk = kk[CHUNK:] * decay

        b_pow = -a_mat
        x_inv = eye + b_pow
        b_pow = _dot_hi(b_pow, b_pow)
        n_sq = int(math.log2(CHUNK)) - 1
        for it in range(n_sq):
            if it < n_sq - 1:
                p = _dot_hi(jnp.concatenate([b_pow, x_inv], axis=0), b_pow)
                b_pow = p[:CHUNK]
                x_inv = x_inv + p[CHUNK:]
            else:
                x_inv = x_inv + _dot_hi(x_inv, b_pow)

        rhs = jnp.concatenate([v * beta, kb * eg], axis=1).astype(BF16)
        uw = _dot(x_inv.astype(BF16), rhs)
        u = uw[:, :GDN_HEAD_DIM]
        w = uw[:, GDN_HEAD_DIM:]
        s = state_ref[h]
        ws_qs = _dot(jnp.concatenate([w, q * eg], axis=0).astype(BF16), s.astype(BF16))
        v_new = u - ws_qs[:CHUNK]
        v_new_b = v_new.astype(BF16)
        out = ws_qs[CHUNK:] + _dot(qk.astype(BF16), v_new_b)
        k_tail_t = (k * etail).T.astype(BF16)
        state_ref[h] = s * eglast + _dot(k_tail_t, v_new_b)

        zh = z_ref[:, lo:lo + GDN_HEAD_DIM]
        o = out * lax.rsqrt(jnp.mean(out * out, axis=-1, keepdims=True) + NORM_EPS) * normw_ref[...]
        o_ref[:, lo:lo + GDN_HEAD_DIM] = (o * _silu(zh)).astype(o_ref.dtype)


def _lane_vec(values, lane0):
    v = jnp.zeros((1, LANES), F32)
    return lax.dynamic_update_slice(v, values.astype(F32).reshape(1, -1), (0, lane0))


def _gdn(proj, small, conv_w, a_log, dt_bias, norm_w):
    bsz, seqlen, _ = proj.shape
    nc = seqlen // CHUNK
    wq = 3 * GDN_WIDTH
    rows_per_halo = CHUNK // HALO
    return pl.pallas_call(
        _gdn_kernel,
        out_shape=jax.ShapeDtypeStruct((bsz, seqlen, GDN_WIDTH), F32),
        grid=(bsz, nc),
        in_specs=[
            pl.BlockSpec((None, CHUNK, wq), lambda b, c: (b, c, OFF_QKV // wq)),
            pl.BlockSpec((None, HALO, wq),
                         lambda b, c: (b, jnp.maximum(c * rows_per_halo - 1, 0), OFF_QKV // wq)),
            pl.BlockSpec((None, CHUNK, GDN_WIDTH), lambda b, c: (b, c, OFF_AZ // GDN_WIDTH)),
            pl.BlockSpec((None, CHUNK, LANES), lambda b, c: (b, c, 0)),
            pl.BlockSpec((CONV_K, wq), lambda b, c: (0, 0)),
            pl.BlockSpec((1, LANES), lambda b, c: (0, 0)),
            pl.BlockSpec((1, LANES), lambda b, c: (0, 0)),
            pl.BlockSpec((1, GDN_HEAD_DIM), lambda b, c: (0, 0)),
        ],
        out_specs=pl.BlockSpec((None, CHUNK, GDN_WIDTH), lambda b, c: (b, c, 0)),
        scratch_shapes=[pltpu.VMEM((GDN_HEADS, GDN_HEAD_DIM, GDN_HEAD_DIM), F32),
                        pltpu.VMEM((HALO + CHUNK, wq), F32)],
        compiler_params=pltpu.CompilerParams(
            dimension_semantics=("parallel", "arbitrary"), vmem_limit_bytes=VMEM_LIMIT),
        name="gdn",
    )(proj, proj, proj, small, conv_w.astype(F32), _lane_vec(a_log, LANE_DECAY),
      _lane_vec(dt_bias, LANE_DECAY), norm_w.astype(F32).reshape(1, GDN_HEAD_DIM))


def _ssd_kernel(x_ref, bc_ref, xprev_ref, bcprev_ref, z_ref, sm_ref, convw_ref, convb_ref,
                alog_ref, dtb_ref, dskip_ref, normw_ref, expand_ref,
                o_ref, state_ref, ext_ref):
    c = pl.program_id(1)
    first = c == 0

    @pl.when(first)
    def _():
        state_ref[...] = jnp.zeros_like(state_ref)

    prev_x = xprev_ref[...]
    prev_bc = bcprev_ref[...]
    ext_ref[0:HALO, 0:M2_WIDTH] = jnp.where(first, jnp.zeros_like(prev_x), prev_x)
    ext_ref[0:HALO, M2_WIDTH:] = jnp.where(first, jnp.zeros_like(prev_bc), prev_bc)
    ext_ref[HALO:, 0:M2_WIDTH] = x_ref[...]
    ext_ref[HALO:, M2_WIDTH:] = bc_ref[...]

    sm = sm_ref[...]
    dt_all = _softplus(sm + dtb_ref[...])
    a_all = -jnp.exp(alog_ref[...]) * dt_all
    causal = _tri(CHUNK)
    acum = _dot_hi(causal.astype(F32), a_all)
    acum_t = acum.T
    ea = jnp.exp(acum)
    alast = acum[CHUNK - 1:CHUNK, :]
    ds = jnp.exp(alast - acum)
    spread = _dot_hi(jnp.concatenate([dt_all, ea, ds], axis=0), expand_ref[...].astype(F32))
    dt_x = spread[:CHUNK]
    ea_x = spread[CHUNK:2 * CHUNK]
    ds_x = spread[2 * CHUNK:]
    cdecay_x = ea_x[CHUNK - 1:CHUNK, :]

    lane = lax.broadcasted_iota(jnp.int32, (CHUNK, 2 * M2_HEAD_DIM), 1)
    heads_per_group = M2_HEADS // M2_GROUPS

    for g in range(M2_GROUPS):
        glo = g * M2_GROUP_W
        bcol = M2_WIDTH + g * M2_STATE
        ccol = M2_WIDTH + M2_BC + g * M2_STATE
        bm = _silu(_conv_tile(ext_ref, convw_ref, bcol, M2_STATE) + convb_ref[:, bcol:bcol + M2_STATE])
        cm = _silu(_conv_tile(ext_ref, convw_ref, ccol, M2_STATE) + convb_ref[:, ccol:ccol + M2_STATE])
        xs = _silu(_conv_tile(ext_ref, convw_ref, glo, M2_GROUP_W) + convb_ref[:, glo:glo + M2_GROUP_W])
        xdt = xs * dt_x[:, glo:glo + M2_GROUP_W]
        xds = (xdt * ds_x[:, glo:glo + M2_GROUP_W]).astype(BF16)
        xdt_b = xdt.astype(BF16)
        bm_b = bm.astype(BF16)
        cm_b = cm.astype(BF16)

        scores = _dot_nt(cm_b, bm_b)
        s_prev = state_ref[g]
        y_off = _dot(cm_b, s_prev.astype(BF16))
        states = _dot(bm.T.astype(BF16), xds)
        state_ref[g] = s_prev * cdecay_x[:, glo:glo + M2_GROUP_W] + states

        parts = []
        for j in range(heads_per_group // 2):
            x2 = xdt_b[:, j * 2 * M2_HEAD_DIM:(j + 1) * 2 * M2_HEAD_DIM]
            yd = []
            for r in range(2):
                la = LANE_DT + g * heads_per_group + 2 * j + r
                seg = jnp.exp(jnp.where(causal, acum[:, la:la + 1] - acum_t[la:la + 1, :], -jnp.inf))
                yd.append(_dot((scores * seg).astype(BF16), x2))
            parts.append(jnp.where(lane < M2_HEAD_DIM, yd[0], yd[1]))
        y = jnp.concatenate(parts, axis=1) + y_off * ea_x[:, glo:glo + M2_GROUP_W]
        y = y + dskip_ref[:, glo:glo + M2_GROUP_W] * xs
        y = y * _silu(z_ref[:, glo:glo + M2_GROUP_W])
        y = y * lax.rsqrt(jnp.mean(y * y, axis=-1, keepdims=True) + NORM_EPS)
        o_ref[:, glo:glo + M2_GROUP_W] = (y * normw_ref[:, glo:glo + M2_GROUP_W]).astype(o_ref.dtype)


def _ssd(proj, small, conv_w, conv_b, a_log, dt_bias, d_skip, norm_w):
    bsz, seqlen, _ = proj.shape
    nc = seqlen // CHUNK
    rows_per_halo = CHUNK // HALO
    conv_dim = M2_WIDTH + 2 * M2_BC
    head_of_col = jnp.arange(M2_WIDTH) // M2_HEAD_DIM
    expand = (jnp.arange(LANES)[:, None] == (LANE_DT + head_of_col)[None, :]).astype(BF16)
    dskip_x = jnp.repeat(d_skip.astype(F32), M2_HEAD_DIM).reshape(1, M2_WIDTH)
    xi = OFF_XBC // M2_WIDTH

    def halo_idx(b, c):
        return jnp.maximum(c * rows_per_halo - 1, 0)

    return pl.pallas_call(
        _ssd_kernel,
        out_shape=jax.ShapeDtypeStruct((bsz, seqlen, M2_WIDTH), F32),
        grid=(bsz, nc),
        in_specs=[
            pl.BlockSpec((None, CHUNK, M2_WIDTH), lambda b, c: (b, c, xi)),
            pl.BlockSpec((None, CHUNK, 2 * M2_BC), lambda b, c: (b, c, xi + 1)),
            pl.BlockSpec((None, HALO, M2_WIDTH), lambda b, c: (b, halo_idx(b, c), xi)),
            pl.BlockSpec((None, HALO, 2 * M2_BC), lambda b, c: (b, halo_idx(b, c), xi + 1)),
            pl.BlockSpec((None, CHUNK, M2_WIDTH), lambda b, c: (b, c, OFF_CZ // M2_WIDTH)),
            pl.BlockSpec((None, CHUNK, LANES), lambda b, c: (b, c, 0)),
            pl.BlockSpec((CONV_K, conv_dim), lambda b, c: (0, 0)),
            pl.BlockSpec((1, conv_dim), lambda b, c: (0, 0)),
            pl.BlockSpec((1, LANES), lambda b, c: (0, 0)),
            pl.BlockSpec((1, LANES), lambda b, c: (0, 0)),
            pl.BlockSpec((1, M2_WIDTH), lambda b, c: (0, 0)),
            pl.BlockSpec((1, M2_WIDTH), lambda b, c: (0, 0)),
            pl.BlockSpec((LANES, M2_WIDTH), lambda b, c: (0, 0)),
        ],
        out_specs=pl.BlockSpec((None, CHUNK, M2_WIDTH), lambda b, c: (b, c, 0)),
        scratch_shapes=[pltpu.VMEM((M2_GROUPS, M2_STATE, M2_GROUP_W), F32),
                        pltpu.VMEM((HALO + CHUNK, conv_dim), F32)],
        compiler_params=pltpu.CompilerParams(
            dimension_semantics=("parallel", "arbitrary"), vmem_limit_bytes=VMEM_LIMIT),
        name="ssd",
    )(proj, proj, proj, proj, proj, small, conv_w.astype(F32), conv_b.astype(F32).reshape(1, conv_dim),
      _lane_vec(a_log, LANE_DT), _lane_vec(dt_bias, LANE_DT), dskip_x,
      norm_w.astype(F32).reshape(1, M2_WIDTH), expand)


def _s5_kernel(u_ref, lamre_c_ref, lamim_c_ref, lamre_r_ref, lamim_r_ref, lstep_ref,
               bre_c_ref, bim_c_ref, bre_r_ref, bim_r_ref, cre_c_ref, cim_c_ref, d_ref,
               o_ref, toep_ref, *, chunks_per_seq):
    p = S5_STATE
    hs = S5_GROUP_SIZE
    cw = CHUNK * hs
    step = jnp.exp(lstep_ref[0:1, :])
    step_r = step[:, :p]

    def disc(lre, lim, st):
        lre = jnp.minimum(lre, -1e-4)
        mag = jnp.exp(lre * st)
        are = mag * jnp.cos(lim * st)
        aim = mag * jnp.sin(lim * st)
        den = lre * lre + lim * lim
        fre = ((are - 1.0) * lre + aim * lim) / den
        fim = (aim * lre - (are - 1.0) * lim) / den
        return lre, are, aim, fre, fim

    lre_c, are_c, aim_c, fre_c, fim_c = disc(lamre_c_ref[...], lamim_c_ref[...], step)
    lim_c = lamim_c_ref[...]
    bbre_c = fre_c * bre_c_ref[...] - fim_c * bim_c_ref[...]
    bbim_c = fre_c * bim_c_ref[...] + fim_c * bre_c_ref[...]
    lre_r, _, _, fre_r, fim_r = disc(lamre_r_ref[0:1, :], lamim_r_ref[0:1, :], step_r)
    lim_r = lamim_r_ref[0:1, :]
    bbre_r = fre_r * bre_r_ref[...] - fim_r * bim_r_ref[...]
    bbim_r = fre_r * bim_r_ref[...] + fim_r * bre_r_ref[...]

    def tile8(x):
        return jnp.concatenate([x] * (cw // LANES), axis=1)

    lane = lax.broadcasted_iota(jnp.int32, (p, cw), 1)
    lag = lax.shift_right_logical(lane, int(math.log2(hs))).astype(F32)

    def powers(expo):
        mag = jnp.exp(tile8(lre_c * step) * expo)
        ang = tile8(lim_c * step) * expo
        return mag * jnp.cos(ang), mag * jnp.sin(ang)

    e0re, e0im = powers(lag)
    cre = tile8(cre_c_ref[...])
    cim = tile8(cim_c_ref[...])
    m0re = cre * e0re - cim * e0im
    m0im = cre * e0im + cim * e0re
    are_t = tile8(are_c)
    aim_t = tile8(aim_c)
    m1re = m0re * are_t - m0im * aim_t
    m1im = m0re * aim_t + m0im * are_t
    ervre, ervim = powers(float(CHUNK - 1) - lag)
    bre_t = tile8(bbre_c)
    bim_t = tile8(bbim_c)
    nre = ervre * bre_t - ervim * bim_t
    nim = ervre * bim_t + ervim * bre_t

    kt = _dot_hi(bbre_r, m0re) - _dot_hi(bbim_r, m0im)
    lane_k = lax.broadcasted_iota(jnp.int32, (hs, cw), 1)

    def fill(s, carry):
        shift = s * hs
        rolled = pltpu.roll(kt, shift, 1)
        blk = jnp.where(lane_k >= shift, rolled, 0.0)
        toep_ref[pl.ds(pl.multiple_of(shift, hs), hs), :] = blk.astype(BF16)
        return carry

    lax.fori_loop(0, CHUNK, fill, 0)

    u = u_ref[...]
    ub = u.astype(BF16)
    y = _dot(ub, toep_ref[...])
    sre = _dot_nt(ub, nre.astype(BF16))
    sim = _dot_nt(ub, nim.astype(BF16))

    rows = u.shape[0]
    cidx = lax.broadcasted_iota(jnp.int32, (rows, p), 0) & (chunks_per_seq - 1)
    mag_c = jnp.exp(lre_r * step_r * float(CHUNK))
    pre = mag_c * jnp.cos(lim_r * step_r * float(CHUNK))
    pim = mag_c * jnp.sin(lim_r * step_r * float(CHUNK))
    d = 1
    while d < chunks_per_seq:
        ok = cidx >= d
        shre = jnp.where(ok, pltpu.roll(sre, d, 0), 0.0)
        shim = jnp.where(ok, pltpu.roll(sim, d, 0), 0.0)
        sre, sim = sre + pre * shre - pim * shim, sim + pre * shim + pim * shre
        pre, pim = pre * pre - pim * pim, 2.0 * pre * pim
        d *= 2
    ok = cidx >= 1
    xsre = jnp.where(ok, pltpu.roll(sre, 1, 0), 0.0)
    xsim = jnp.where(ok, pltpu.roll(sim, 1, 0), 0.0)
    y = y + _dot(xsre.astype(BF16), m1re.astype(BF16)) - _dot(xsim.astype(BF16), m1im.astype(BF16))
    o_ref[...] = (y + d_ref[...] * u).astype(o_ref.dtype)


def _s5(u3, lam_re, lam_im, log_step, b_re, b_im, c_re, c_im, d_skip, chunks_per_seq):
    g, rows, cw = u3.shape
    p, hs = S5_STATE, S5_GROUP_SIZE
    rep = LANES // hs
    f32 = lambda a: a.astype(F32)
    col = lambda a: jnp.broadcast_to(f32(a)[:, :, None], (g, p, LANES))
    row = lambda a: jnp.broadcast_to(f32(a)[:, None, :], (g, HALO, p))
    lstep = jnp.broadcast_to(f32(log_step)[:, None, None], (g, HALO, LANES))
    b_c = lambda a: jnp.tile(f32(a), (1, 1, rep))
    b_r = lambda a: jnp.swapaxes(f32(a), 1, 2)
    c_c = lambda a: jnp.tile(jnp.swapaxes(f32(a), 1, 2), (1, 1, rep))
    d_t = jnp.tile(f32(d_skip), (1, CHUNK)).reshape(g, 1, cw)

    def gspec(shape):
        return pl.BlockSpec((None,) + shape, lambda i: (i, 0, 0))

    return pl.pallas_call(
        functools.partial(_s5_kernel, chunks_per_seq=chunks_per_seq),
        out_shape=jax.ShapeDtypeStruct((g, rows, cw), F32),
        grid=(g,),
        in_specs=[gspec((rows, cw)),
                  gspec((p, LANES)), gspec((p, LANES)), gspec((HALO, p)), gspec((HALO, p)),
                  gspec((HALO, LANES)),
                  gspec((p, LANES)), gspec((p, LANES)), gspec((hs, p)), gspec((hs, p)),
                  gspec((p, LANES)), gspec((p, LANES)), gspec((1, cw))],
        out_specs=gspec((rows, cw)),
        scratch_shapes=[pltpu.VMEM((cw, cw), BF16)],
        compiler_params=pltpu.CompilerParams(
            dimension_semantics=("parallel",), vmem_limit_bytes=VMEM_LIMIT),
        name="s5",
    )(u3, col(lam_re), col(lam_im), row(lam_re), row(lam_im), lstep,
      b_c(b_re), b_c(b_im), b_r(b_re), b_r(b_im), c_c(c_re), c_c(c_im), d_t)


def _merge_kernel(x_ref, gate_ref, ya_ref, ys_ref, sg_ref, yc_ref,
                  gluw_ref, glub_ref, pa_ref, pb_ref, pc_ref, wout_ref, fnw_ref, o_ref, *, final):
    d = x_ref.shape[-1]
    ys = jax.nn.gelu(ys_ref[...])
    glu = jax.nn.sigmoid(_dot(ys.astype(BF16), gluw_ref[...]) + glub_ref[...])
    yb = ys * glu * _silu(sg_ref[...])
    pa = _dot(ya_ref[...].astype(BF16), pa_ref[...])
    pb = _dot(yb.astype(BF16), pb_ref[...])
    pc = _dot(yc_ref[...].astype(BF16), pc_ref[...])
    merged = (jax.nn.sigmoid(gate_ref[:, 0:d]) * pa
              + jax.nn.sigmoid(gate_ref[:, d:2 * d]) * pb
              + jax.nn.sigmoid(gate_ref[:, 2 * d:3 * d]) * pc)
    out = x_ref[...] + _dot(merged.astype(BF16), wout_ref[...])
    if final:
        out = out * lax.rsqrt(jnp.mean(out * out, axis=-1, keepdims=True) + NORM_EPS) * fnw_ref[...]
    o_ref[...] = out.astype(o_ref.dtype)


def _merge(x2d, proj2d, ya, ys, yc, glu_w, glu_b, proj_a, proj_b, proj_c, w_out, final_w, final, tm=256):
    t, d = x2d.shape
    tm = min(tm, t)
    sgw = S5_WIDTH

    def wspec(shape):
        return pl.BlockSpec(shape, lambda i: (0, 0), pipeline_mode=pl.Buffered(1))

    return pl.pallas_call(
        functools.partial(_merge_kernel, final=final),
        out_shape=jax.ShapeDtypeStruct((t, d), F32),
        grid=(t // tm,),
        in_specs=[
            pl.BlockSpec((tm, d), lambda i: (i, 0)),
            pl.BlockSpec((tm, 3 * d), lambda i: (i, OFF_MERGE // (3 * d))),
            pl.BlockSpec((tm, GDN_WIDTH), lambda i: (i, 0)),
            pl.BlockSpec((tm, S5_WIDTH), lambda i: (i, 0)),
            pl.BlockSpec((tm, sgw), lambda i: (i, 0)),
            pl.BlockSpec((tm, M2_WIDTH), lambda i: (i, 0)),
            wspec((S5_WIDTH, S5_WIDTH)), wspec((1, S5_WIDTH)),
            wspec((GDN_WIDTH, d)), wspec((S5_WIDTH, d)), wspec((M2_WIDTH, d)), wspec((d, d)),
            wspec((1, d)),
        ],
        out_specs=pl.BlockSpec((tm, d), lambda i: (i, 0)),
        compiler_params=pltpu.CompilerParams(
            dimension_semantics=("parallel",), vmem_limit_bytes=VMEM_LIMIT),
        name="merge",
    )(x2d, proj2d, ya, ys, proj2d[:, OFF_SG:OFF_SG + sgw], yc,
      glu_w.astype(BF16), glu_b.astype(F32).reshape(1, -1),
      proj_a.astype(BF16), proj_b.astype(BF16), proj_c.astype(BF16), w_out.astype(BF16),
      final_w.astype(F32).reshape(1, d))


def _pack_w_in(w):
    gw, sw, mw = GDN_WIDTH, S5_WIDTH, M2_WIDTH
    conv_dim = mw + 2 * M2_BC
    o = 0
    seg = {}
    for name, width in (("qkv", 3 * gw), ("az", gw), ("beta", GDN_HEADS), ("decay", GDN_HEADS),
                        ("su", sw), ("sg", sw), ("cz", mw), ("xbc", conv_dim), ("dt", M2_HEADS),
                        ("merge", 3 * w.shape[0])):
        seg[name] = w[:, o:o + width]
        o += width
    main = jnp.concatenate([seg["merge"], seg["qkv"], seg["xbc"], seg["az"], seg["cz"], seg["su"], seg["sg"]],
                           axis=1)
    pad = jnp.zeros((w.shape[0], LANES - 2 * GDN_HEADS - M2_HEADS), w.dtype)
    small = jnp.concatenate([seg["beta"], seg["decay"], seg["dt"], pad], axis=1)
    return main.astype(BF16), small.astype(BF16)


def _layer(x2d, bsz, seqlen, norm_w, w_in, gdn_conv_w, gdn_a_log, gdn_dt_bias, gdn_norm_w,
           s5_lam_re, s5_lam_im, s5_log_step, s5_b_re, s5_b_im, s5_c_re, s5_c_im, s5_d,
           s5_glu_w, s5_glu_b, m2_conv_w, m2_conv_b, m2_a_log, m2_dt_bias, m2_d, m2_norm_w,
           proj_a, proj_b, proj_c, w_out, final_w, final):
    t = bsz * seqlen
    nc = seqlen // CHUNK
    w_main, w_small = _pack_w_in(w_in)
    h = _rmsnorm(x2d, norm_w, BF16)
    proj = _matmul(h, w_main, 1024, 512)
    small = _matmul(h, w_small, 1024, LANES)
    proj3 = proj.reshape(bsz, seqlen, MAIN_W)
    small3 = small.reshape(bsz, seqlen, LANES)

    ya = _gdn(proj3, small3, gdn_conv_w, gdn_a_log, gdn_dt_bias, gdn_norm_w)
    yc = _ssd(proj3, small3, m2_conv_w, m2_conv_b, m2_a_log, m2_dt_bias, m2_d, m2_norm_w)

    u = proj[:, OFF_SU:OFF_SU + S5_WIDTH].reshape(bsz * nc, CHUNK, S5_GROUPS, S5_GROUP_SIZE)
    u = jnp.transpose(u, (2, 0, 1, 3)).reshape(S5_GROUPS, bsz * nc, CHUNK * S5_GROUP_SIZE)
    ys = _s5(u, s5_lam_re, s5_lam_im, s5_log_step, s5_b_re, s5_b_im, s5_c_re, s5_c_im, s5_d, nc)
    ys = ys.reshape(S5_GROUPS, bsz * nc, CHUNK, S5_GROUP_SIZE)
    ys = jnp.transpose(ys, (1, 2, 0, 3)).reshape(t, S5_WIDTH)

    return _merge(x2d, proj, ya.reshape(t, GDN_WIDTH), ys, yc.reshape(t, M2_WIDTH),
                  s5_glu_w, s5_glu_b, proj_a, proj_b, proj_c, w_out, final_w, final)


def kernel(x, norm_w, w_in, gdn_conv_w, gdn_a_log, gdn_dt_bias, gdn_norm_w, s5_lam_re, s5_lam_im, s5_log_step, s5_b_re, s5_b_im, s5_c_re, s5_c_im, s5_d, s5_glu_w, s5_glu_b, m2_conv_w, m2_conv_b, m2_a_log, m2_dt_bias, m2_d, m2_norm_w, proj_a, proj_b, proj_c, w_out, final_norm_w):
    bsz, seqlen, d = x.shape
    depth = norm_w.shape[0]
    x2d = x.reshape(bsz * seqlen, d)
    for i in range(depth):
        x2d = _layer(x2d, bsz, seqlen, norm_w[i], w_in[i], gdn_conv_w[i], gdn_a_log[i], gdn_dt_bias[i],
                     gdn_norm_w[i], s5_lam_re[i], s5_lam_im[i], s5_log_step[i], s5_b_re[i], s5_b_im[i],
                     s5_c_re[i], s5_c_im[i], s5_d[i], s5_glu_w[i], s5_glu_b[i],
                     m2_conv_w[i], m2_conv_b[i], m2_a_log[i], m2_dt_bias[i], m2_d[i], m2_norm_w[i],
                     proj_a[i], proj_b[i], proj_c[i], w_out[i], final_norm_w, i == depth - 1)
    return x2d.reshape(bsz, seqlen, d)
```

```python
import functools
import math

import jax
import jax.numpy as jnp
from jax import lax
from jax.experimental import pallas as pl
from jax.experimental.pallas import tpu as pltpu

F32 = jnp.float32
BF16 = jnp.bfloat16
HIGHEST = lax.Precision.HIGHEST

NORM_EPS = 1e-6
CHUNK = 64
CONV_K = 4
HALO = 8
LANES = 128

GDN_HEADS = 8
GDN_HEAD_DIM = 128
GDN_WIDTH = GDN_HEADS * GDN_HEAD_DIM
S5_GROUP_SIZE = 16
S5_GROUPS = 48
S5_STATE = 64
S5_WIDTH = S5_GROUPS * S5_GROUP_SIZE
M2_HEADS = 16
M2_HEAD_DIM = 64
M2_WIDTH = M2_HEADS * M2_HEAD_DIM
M2_GROUPS = 4
M2_STATE = 128
M2_BC = M2_GROUPS * M2_STATE
M2_GROUP_W = M2_WIDTH // M2_GROUPS

OFF_MERGE = 0
OFF_QKV = OFF_MERGE + 3 * 2048
OFF_XBC = OFF_QKV + 3 * GDN_WIDTH
OFF_AZ = OFF_XBC + M2_WIDTH + 2 * M2_BC
OFF_CZ = OFF_AZ + GDN_WIDTH
OFF_SU = OFF_CZ + M2_WIDTH
OFF_SG = OFF_SU + S5_WIDTH
MAIN_W = OFF_SG + S5_WIDTH
LANE_BETA = 0
LANE_DECAY = GDN_HEADS
LANE_DT = 2 * GDN_HEADS

VMEM_LIMIT = 56 * 1024 * 1024


def _softplus(x):
    return jnp.maximum(x, 0.0) + jnp.log(1.0 + jnp.exp(-jnp.abs(x)))


def _silu(x):
    return x * jax.nn.sigmoid(x)


def _dot(a, b):
    return jnp.dot(a, b, preferred_element_type=F32)


def _dot_nt(a, b):
    return lax.dot_general(a, b, (((1,), (1,)), ((), ())), preferred_element_type=F32)


def _dot_hi(a, b):
    return jnp.dot(a, b, preferred_element_type=F32, precision=HIGHEST)


def _dot_split(a, b):
    a_hi = a.astype(BF16)
    a_lo = (a - a_hi.astype(F32)).astype(BF16)
    b_hi = b.astype(BF16)
    b_lo = (b - b_hi.astype(F32)).astype(BF16)
    m = a.shape[0]
    top = _dot(jnp.concatenate([a_hi, a_lo], axis=0), b_hi)
    return top[:m] + top[m:] + _dot(a_hi, b_lo)


def _tri(n, strict=False):
    r = lax.broadcasted_iota(jnp.int32, (n, n), 0)
    c = lax.broadcasted_iota(jnp.int32, (n, n), 1)
    return (r > c) if strict else (r >= c)


def _rmsnorm_kernel(x_ref, w_ref, o_ref):
    x = x_ref[...]
    y = x * lax.rsqrt(jnp.mean(x * x, axis=-1, keepdims=True) + NORM_EPS)
    o_ref[...] = (y * w_ref[...]).astype(o_ref.dtype)


def _rmsnorm(x2d, w, out_dtype, tm=512):
    t, d = x2d.shape
    tm = min(tm, t)
    return pl.pallas_call(
        _rmsnorm_kernel,
        out_shape=jax.ShapeDtypeStruct((t, d), out_dtype),
        grid=(t // tm,),
        in_specs=[pl.BlockSpec((tm, d), lambda i: (i, 0)),
                  pl.BlockSpec((1, d), lambda i: (0, 0))],
        out_specs=pl.BlockSpec((tm, d), lambda i: (i, 0)),
        compiler_params=pltpu.CompilerParams(dimension_semantics=("parallel",)),
        name="rmsnorm",
    )(x2d, w.reshape(1, d))


def _matmul_kernel(a_ref, b_ref, o_ref):
    o_ref[...] = _dot(a_ref[...], b_ref[...]).astype(o_ref.dtype)


def _matmul(a, b, tm, tn, out_dtype=F32):
    m, k = a.shape
    _, n = b.shape
    tm = min(tm, m)
    tn = min(tn, n)
    return pl.pallas_call(
        _matmul_kernel,
        out_shape=jax.ShapeDtypeStruct((m, n), out_dtype),
        grid=(m // tm, n // tn),
        in_specs=[pl.BlockSpec((tm, k), lambda i, j: (i, 0)),
                  pl.BlockSpec((k, tn), lambda i, j: (0, j))],
        out_specs=pl.BlockSpec((tm, tn), lambda i, j: (i, j)),
        compiler_params=pltpu.CompilerParams(
            dimension_semantics=("parallel", "parallel"), vmem_limit_bytes=VMEM_LIMIT),
        name="inproj",
    )(a, b)


def _fill_ext(ext_ref, prev_ref, cur_ref, first):
    prev = prev_ref[...]
    ext_ref[0:HALO, :] = jnp.where(first, jnp.zeros_like(prev), prev)
    ext_ref[HALO:HALO + CHUNK, :] = cur_ref[...]


def _conv_tile(ext_ref, w_ref, col, width):
    acc = None
    for k in range(CONV_K):
        start = HALO - (CONV_K - 1) + k
        term = w_ref[k:k + 1, col:col + width] * ext_ref[start:start + CHUNK, col:col + width]
        acc = term if acc is None else acc + term
    return acc


def _gdn_kernel(q_ref, qprev_ref, z_ref, sm_ref, convw_ref, alog_ref, dtb_ref, normw_ref,
                o_ref, state_ref, ext_ref):
    c = pl.program_id(1)
    first = c == 0

    @pl.when(first)
    def _():
        state_ref[...] = jnp.zeros_like(state_ref)

    _fill_ext(ext_ref, qprev_ref, q_ref, first)

    sm = sm_ref[...]
    beta_all = jax.nn.sigmoid(sm)
    g_all = -jnp.exp(alog_ref[...]) * _softplus(sm + dtb_ref[...])
    causal = _tri(CHUNK)
    strict = _tri(CHUNK, strict=True)
    ltri = causal.astype(F32)
    gc_all = _dot_hi(ltri, g_all)
    gc_t = gc_all.T
    eg_all = jnp.exp(gc_all)
    glast_all = gc_all[CHUNK - 1:CHUNK, :]
    etail_all = jnp.exp(glast_all - gc_all)
    eglast_all = jnp.exp(glast_all)
    eye = (lax.broadcasted_iota(jnp.int32, (CHUNK, CHUNK), 0)
           == lax.broadcasted_iota(jnp.int32, (CHUNK, CHUNK), 1)).astype(F32)
    scale = GDN_HEAD_DIM ** -0.5

    heads = range(GDN_HEADS)
    dk = GDN_HEAD_DIM

    def prep(h):
        lo = h * dk
        q = _silu(_conv_tile(ext_ref, convw_ref, lo, dk))
        k = _silu(_conv_tile(ext_ref, convw_ref, GDN_WIDTH + lo, dk))
        v = _silu(_conv_tile(ext_ref, convw_ref, 2 * GDN_WIDTH + lo, dk))
        q = q * lax.rsqrt(jnp.sum(q * q, axis=-1, keepdims=True) + NORM_EPS) * scale
        k = k * lax.rsqrt(jnp.sum(k * k, axis=-1, keepdims=True) + NORM_EPS)
        beta = beta_all[:, LANE_BETA + h:LANE_BETA + h + 1]
        ld = LANE_DECAY + h
        eg = eg_all[:, ld:ld + 1]
        decay = jnp.exp(jnp.where(causal, gc_all[:, ld:ld + 1] - gc_t[ld:ld + 1, :], -jnp.inf))
        kb = k * beta
        return dict(
            lhs_kk=jnp.concatenate([kb, q], axis=0).astype(BF16), k_b=k.astype(BF16), decay=decay,
            rhs_uw=jnp.concatenate([v * beta, kb * eg], axis=1).astype(BF16),
            q_dec=(q * eg).astype(BF16),
            k_tail_t=(k * etail_all[:, ld:ld + 1]).T.astype(BF16),
            eglast=eglast_all[:, ld:ld + 1])

    hd = [prep(h) for h in heads]
    kk = [_dot_nt(d["lhs_kk"], d["k_b"]) for d in hd]
    qk = [(kk[h][CHUNK:] * hd[h]["decay"]).astype(BF16) for h in heads]

    b_pow = [-jnp.where(strict, kk[h][:CHUNK] * hd[h]["decay"], 0.0) for h in heads]
    x_inv = [eye + b for b in b_pow]
    b_pow = [_dot_split(b, b) for b in b_pow]
    n_sq = int(math.log2(CHUNK)) - 1
    for it in range(n_sq):
        if it < n_sq - 1:
            p = [_dot_split(jnp.concatenate([b_pow[h], x_inv[h]], axis=0), b_pow[h]) for h in heads]
            b_pow = [p[h][:CHUNK] for h in heads]
            x_inv = [x_inv[h] + p[h][CHUNK:] for h in heads]
        else:
            x_inv = [x_inv[h] + _dot_split(x_inv[h], b_pow[h]) for h in heads]

    uw = [_dot(x_inv[h].astype(BF16), hd[h]["rhs_uw"]) for h in heads]
    s_old = [state_ref[h] for h in heads]
    ws_qs = [_dot(jnp.concatenate([uw[h][:, dk:].astype(BF16), hd[h]["q_dec"]], axis=0),
                  s_old[h].astype(BF16)) for h in heads]
    v_new = [(uw[h][:, :dk] - ws_qs[h][:CHUNK]).astype(BF16) for h in heads]
    out = [ws_qs[h][CHUNK:] + _dot(qk[h], v_new[h]) for h in heads]
    for h in heads:
        state_ref[h] = s_old[h] * hd[h]["eglast"] + _dot(hd[h]["k_tail_t"], v_new[h])
    for h in heads:
        lo = h * dk
        o = out[h] * lax.rsqrt(jnp.mean(out[h] * out[h], axis=-1, keepdims=True) + NORM_EPS) * normw_ref[...]
        o_ref[:, lo:lo + dk] = (o * _silu(z_ref[:, lo:lo + dk])).astype(o_ref.dtype)


def _lane_vec(values, lane0):
    v = jnp.zeros((1, LANES), F32)
    return lax.dynamic_update_slice(v, values.astype(F32).reshape(1, -1), (0, lane0))


def _gdn(proj, small, conv_w, a_log, dt_bias, norm_w):
    bsz, seqlen, _ = proj.shape
    nc = seqlen // CHUNK
    wq = 3 * GDN_WIDTH
    rows_per_halo = CHUNK // HALO
    return pl.pallas_call(
        _gdn_kernel,
        out_shape=jax.ShapeDtypeStruct((bsz, seqlen, GDN_WIDTH), F32),
        grid=(bsz, nc),
        in_specs=[
            pl.BlockSpec((None, CHUNK, wq), lambda b, c: (b, c, OFF_QKV // wq)),
            pl.BlockSpec((None, HALO, wq),
                         lambda b, c: (b, jnp.maximum(c * rows_per_halo - 1, 0), OFF_QKV // wq)),
            pl.BlockSpec((None, CHUNK, GDN_WIDTH), lambda b, c: (b, c, OFF_AZ // GDN_WIDTH)),
            pl.BlockSpec((None, CHUNK, LANES), lambda b, c: (b, c, 0)),
            pl.BlockSpec((CONV_K, wq), lambda b, c: (0, 0)),
            pl.BlockSpec((1, LANES), lambda b, c: (0, 0)),
            pl.BlockSpec((1, LANES), lambda b, c: (0, 0)),
            pl.BlockSpec((1, GDN_HEAD_DIM), lambda b, c: (0, 0)),
        ],
        out_specs=pl.BlockSpec((None, CHUNK, GDN_WIDTH), lambda b, c: (b, c, 0)),
        scratch_shapes=[pltpu.VMEM((GDN_HEADS, GDN_HEAD_DIM, GDN_HEAD_DIM), F32),
                        pltpu.VMEM((HALO + CHUNK, wq), F32)],
        compiler_params=pltpu.CompilerParams(
            dimension_semantics=("parallel", "arbitrary"), vmem_limit_bytes=VMEM_LIMIT),
        name="gdn",
    )(proj, proj, proj, small, conv_w.astype(F32), _lane_vec(a_log, LANE_DECAY),
      _lane_vec(dt_bias, LANE_DECAY), norm_w.astype(F32).reshape(1, GDN_HEAD_DIM))


def _ssd_kernel(x_ref, bc_ref, xprev_ref, bcprev_ref, z_ref, sm_ref, convw_ref, convb_ref,
                alog_ref, dtb_ref, dskip_ref, normw_ref, expand_ref,
                o_ref, state_ref, ext_ref):
    c = pl.program_id(1)
    first = c == 0

    @pl.when(first)
    def _():
        state_ref[...] = jnp.zeros_like(state_ref)

    prev_x = xprev_ref[...]
    prev_bc = bcprev_ref[...]
    ext_ref[0:HALO, 0:M2_WIDTH] = jnp.where(first, jnp.zeros_like(prev_x), prev_x)
    ext_ref[0:HALO, M2_WIDTH:] = jnp.where(first, jnp.zeros_like(prev_bc), prev_bc)
    ext_ref[HALO:, 0:M2_WIDTH] = x_ref[...]
    ext_ref[HALO:, M2_WIDTH:] = bc_ref[...]

    sm = sm_ref[...]
    dt_all = _softplus(sm + dtb_ref[...])
    a_all = -jnp.exp(alog_ref[...]) * dt_all
    causal = _tri(CHUNK)
    acum = _dot_hi(causal.astype(F32), a_all)
    acum_t = acum.T
    ea = jnp.exp(acum)
    alast = acum[CHUNK - 1:CHUNK, :]
    ds = jnp.exp(alast - acum)
    spread = _dot_hi(jnp.concatenate([dt_all, ea, ds], axis=0), expand_ref[...].astype(F32))
    dt_x = spread[:CHUNK]
    ea_x = spread[CHUNK:2 * CHUNK]
    ds_x = spread[2 * CHUNK:]
    cdecay_x = ea_x[CHUNK - 1:CHUNK, :]

    lane = lax.broadcasted_iota(jnp.int32, (CHUNK, 2 * M2_HEAD_DIM), 1)
    heads_per_group = M2_HEADS // M2_GROUPS

    groups = range(M2_GROUPS)

    def prep(g):
        glo = g * M2_GROUP_W
        bcol = M2_WIDTH + g * M2_STATE
        ccol = M2_WIDTH + M2_BC + g * M2_STATE
        bm = _silu(_conv_tile(ext_ref, convw_ref, bcol, M2_STATE) + convb_ref[:, bcol:bcol + M2_STATE])
        cm = _silu(_conv_tile(ext_ref, convw_ref, ccol, M2_STATE) + convb_ref[:, ccol:ccol + M2_STATE])
        xs = _silu(_conv_tile(ext_ref, convw_ref, glo, M2_GROUP_W) + convb_ref[:, glo:glo + M2_GROUP_W])
        xdt = xs * dt_x[:, glo:glo + M2_GROUP_W]
        return dict(xs=xs, xdt_b=xdt.astype(BF16), xds=(xdt * ds_x[:, glo:glo + M2_GROUP_W]).astype(BF16),
                    bm_b=bm.astype(BF16), bm_t=bm.T.astype(BF16), cm_b=cm.astype(BF16))

    gd = [prep(g) for g in groups]
    scores = [_dot_nt(d["cm_b"], d["bm_b"]) for d in gd]
    s_prev = [state_ref[g] for g in groups]
    y_off = [_dot(gd[g]["cm_b"], s_prev[g].astype(BF16)) for g in groups]
    states = [_dot(d["bm_t"], d["xds"]) for d in gd]
    for g in groups:
        glo = g * M2_GROUP_W
        state_ref[g] = s_prev[g] * cdecay_x[:, glo:glo + M2_GROUP_W] + states[g]

    def diag(g, j, r):
        la = LANE_DT + g * heads_per_group + 2 * j + r
        seg = jnp.exp(jnp.where(causal, acum[:, la:la + 1] - acum_t[la:la + 1, :], -jnp.inf))
        x2 = gd[g]["xdt_b"][:, j * 2 * M2_HEAD_DIM:(j + 1) * 2 * M2_HEAD_DIM]
        return _dot((scores[g] * seg).astype(BF16), x2)

    pairs = range(heads_per_group // 2)
    yd = [[[diag(g, j, r) for r in range(2)] for j in pairs] for g in groups]
    for g in groups:
        glo = g * M2_GROUP_W
        parts = [jnp.where(lane < M2_HEAD_DIM, yd[g][j][0], yd[g][j][1]) for j in pairs]
        y = jnp.concatenate(parts, axis=1) + y_off[g] * ea_x[:, glo:glo + M2_GROUP_W]
        y = y + dskip_ref[:, glo:glo + M2_GROUP_W] * gd[g]["xs"]
        y = y * _silu(z_ref[:, glo:glo + M2_GROUP_W])
        y = y * lax.rsqrt(jnp.mean(y * y, axis=-1, keepdims=True) + NORM_EPS)
        o_ref[:, glo:glo + M2_GROUP_W] = (y * normw_ref[:, glo:glo + M2_GROUP_W]).astype(o_ref.dtype)


def _ssd(proj, small, conv_w, conv_b, a_log, dt_bias, d_skip, norm_w):
    bsz, seqlen, _ = proj.shape
    nc = seqlen // CHUNK
    rows_per_halo = CHUNK // HALO
    conv_dim = M2_WIDTH + 2 * M2_BC
    head_of_col = jnp.arange(M2_WIDTH) // M2_HEAD_DIM
    expand = (jnp.arange(LANES)[:, None] == (LANE_DT + head_of_col)[None, :]).astype(BF16)
    dskip_x = jnp.repeat(d_skip.astype(F32), M2_HEAD_DIM).reshape(1, M2_WIDTH)
    xi = OFF_XBC // M2_WIDTH

    def halo_idx(b, c):
        return jnp.maximum(c * rows_per_halo - 1, 0)

    return pl.pallas_call(
        _ssd_kernel,
        out_shape=jax.ShapeDtypeStruct((bsz, seqlen, M2_WIDTH), F32),
        grid=(bsz, nc),
        in_specs=[
            pl.BlockSpec((None, CHUNK, M2_WIDTH), lambda b, c: (b, c, xi)),
            pl.BlockSpec((None, CHUNK, 2 * M2_BC), lambda b, c: (b, c, xi + 1)),
            pl.BlockSpec((None, HALO, M2_WIDTH), lambda b, c: (b, halo_idx(b, c), xi)),
            pl.BlockSpec((None, HALO, 2 * M2_BC), lambda b, c: (b, halo_idx(b, c), xi + 1)),
            pl.BlockSpec((None, CHUNK, M2_WIDTH), lambda b, c: (b, c, OFF_CZ // M2_WIDTH)),
            pl.BlockSpec((None, CHUNK, LANES), lambda b, c: (b, c, 0)),
            pl.BlockSpec((CONV_K, conv_dim), lambda b, c: (0, 0)),
            pl.BlockSpec((1, conv_dim), lambda b, c: (0, 0)),
            pl.BlockSpec((1, LANES), lambda b, c: (0, 0)),
            pl.BlockSpec((1, LANES), lambda b, c: (0, 0)),
            pl.BlockSpec((1, M2_WIDTH), lambda b, c: (0, 0)),
            pl.BlockSpec((1, M2_WIDTH), lambda b, c: (0, 0)),
            pl.BlockSpec((LANES, M2_WIDTH), lambda b, c: (0, 0)),
        ],
        out_specs=pl.BlockSpec((None, CHUNK, M2_WIDTH), lambda b, c: (b, c, 0)),
        scratch_shapes=[pltpu.VMEM((M2_GROUPS, M2_STATE, M2_GROUP_W), F32),
                        pltpu.VMEM((HALO + CHUNK, conv_dim), F32)],
        compiler_params=pltpu.CompilerParams(
            dimension_semantics=("parallel", "arbitrary"), vmem_limit_bytes=VMEM_LIMIT),
        name="ssd",
    )(proj, proj, proj, proj, proj, small, conv_w.astype(F32), conv_b.astype(F32).reshape(1, conv_dim),
      _lane_vec(a_log, LANE_DT), _lane_vec(dt_bias, LANE_DT), dskip_x,
      norm_w.astype(F32).reshape(1, M2_WIDTH), expand)


def _s5_kernel(u_ref, lamre_c_ref, lamim_c_ref, lamre_r_ref, lamim_r_ref, lstep_ref,
               bre_c_ref, bim_c_ref, bre_r_ref, bim_r_ref, cre_c_ref, cim_c_ref, d_ref,
               o_ref, toep_ref, *, chunks_per_seq):
    p = S5_STATE
    hs = S5_GROUP_SIZE
    cw = CHUNK * hs
    step = jnp.exp(lstep_ref[0:1, :])
    step_r = step[:, :p]

    def disc(lre, lim, st):
        lre = jnp.minimum(lre, -1e-4)
        mag = jnp.exp(lre * st)
        are = mag * jnp.cos(lim * st)
        aim = mag * jnp.sin(lim * st)
        den = lre * lre + lim * lim
        fre = ((are - 1.0) * lre + aim * lim) / den
        fim = (aim * lre - (are - 1.0) * lim) / den
        return lre, are, aim, fre, fim

    lre_c, are_c, aim_c, fre_c, fim_c = disc(lamre_c_ref[...], lamim_c_ref[...], step)
    lim_c = lamim_c_ref[...]
    bbre_c = fre_c * bre_c_ref[...] - fim_c * bim_c_ref[...]
    bbim_c = fre_c * bim_c_ref[...] + fim_c * bre_c_ref[...]
    lre_r, _, _, fre_r, fim_r = disc(lamre_r_ref[0:1, :], lamim_r_ref[0:1, :], step_r)
    lim_r = lamim_r_ref[0:1, :]
    bbre_r = fre_r * bre_r_ref[...] - fim_r * bim_r_ref[...]
    bbim_r = fre_r * bim_r_ref[...] + fim_r * bre_r_ref[...]

    def tile8(x):
        return jnp.concatenate([x] * (cw // LANES), axis=1)

    lane = lax.broadcasted_iota(jnp.int32, (p, cw), 1)
    lag = lax.shift_right_logical(lane, int(math.log2(hs))).astype(F32)

    def powers(expo):
        mag = jnp.exp(tile8(lre_c * step) * expo)
        ang = tile8(lim_c * step) * expo
        return mag * jnp.cos(ang), mag * jnp.sin(ang)

    e0re, e0im = powers(lag)
    cre = tile8(cre_c_ref[...])
    cim = tile8(cim_c_ref[...])
    m0re = cre * e0re - cim * e0im
    m0im = cre * e0im + cim * e0re
    are_t = tile8(are_c)
    aim_t = tile8(aim_c)
    m1re = m0re * are_t - m0im * aim_t
    m1im = m0re * aim_t + m0im * are_t
    ervre, ervim = powers(float(CHUNK - 1) - lag)
    bre_t = tile8(bbre_c)
    bim_t = tile8(bbim_c)
    nre = ervre * bre_t - ervim * bim_t
    nim = ervre * bim_t + ervim * bre_t

    kt = _dot_hi(bbre_r, m0re) - _dot_hi(bbim_r, m0im)
    lane_k = lax.broadcasted_iota(jnp.int32, (hs, cw), 1)

    def fill(s, carry):
        shift = s * hs
        rolled = pltpu.roll(kt, shift, 1)
        blk = jnp.where(lane_k >= shift, rolled, 0.0)
        toep_ref[pl.ds(pl.multiple_of(shift, hs), hs), :] = blk.astype(BF16)
        return carry

    lax.fori_loop(0, CHUNK, fill, 0)

    u = u_ref[...]
    ub = u.astype(BF16)
    y = _dot(ub, toep_ref[...])
    sre = _dot_nt(ub, nre.astype(BF16))
    sim = _dot_nt(ub, nim.astype(BF16))

    rows = u.shape[0]
    cidx = lax.broadcasted_iota(jnp.int32, (rows, p), 0) & (chunks_per_seq - 1)
    mag_c = jnp.exp(lre_r * step_r * float(CHUNK))
    pre = mag_c * jnp.cos(lim_r * step_r * float(CHUNK))
    pim = mag_c * jnp.sin(lim_r * step_r * float(CHUNK))
    d = 1
    while d < chunks_per_seq:
        ok = cidx >= d
        shre = jnp.where(ok, pltpu.roll(sre, d, 0), 0.0)
        shim = jnp.where(ok, pltpu.roll(sim, d, 0), 0.0)
        sre, sim = sre + pre * shre - pim * shim, sim + pre * shim + pim * shre
        pre, pim = pre * pre - pim * pim, 2.0 * pre * pim
        d *= 2
    ok = cidx >= 1
    xsre = jnp.where(ok, pltpu.roll(sre, 1, 0), 0.0)
    xsim = jnp.where(ok, pltpu.roll(sim, 1, 0), 0.0)
    y = y + _dot(xsre.astype(BF16), m1re.astype(BF16)) - _dot(xsim.astype(BF16), m1im.astype(BF16))
    o_ref[...] = (y + d_ref[...] * u).astype(o_ref.dtype)


def _s5(u3, lam_re, lam_im, log_step, b_re, b_im, c_re, c_im, d_skip, chunks_per_seq):
    g, rows, cw = u3.shape
    p, hs = S5_STATE, S5_GROUP_SIZE
    rep = LANES // hs
    f32 = lambda a: a.astype(F32)
    col = lambda a: jnp.broadcast_to(f32(a)[:, :, None], (g, p, LANES))
    row = lambda a: jnp.broadcast_to(f32(a)[:, None, :], (g, HALO, p))
    lstep = jnp.broadcast_to(f32(log_step)[:, None, None], (g, HALO, LANES))
    b_c = lambda a: jnp.tile(f32(a), (1, 1, rep))
    b_r = lambda a: jnp.swapaxes(f32(a), 1, 2)
    c_c = lambda a: jnp.tile(jnp.swapaxes(f32(a), 1, 2), (1, 1, rep))
    d_t = jnp.tile(f32(d_skip), (1, CHUNK)).reshape(g, 1, cw)

    def gspec(shape):
        return pl.BlockSpec((None,) + shape, lambda i: (i, 0, 0))

    return pl.pallas_call(
        functools.partial(_s5_kernel, chunks_per_seq=chunks_per_seq),
        out_shape=jax.ShapeDtypeStruct((g, rows, cw), F32),
        grid=(g,),
        in_specs=[gspec((rows, cw)),
                  gspec((p, LANES)), gspec((p, LANES)), gspec((HALO, p)), gspec((HALO, p)),
                  gspec((HALO, LANES)),
                  gspec((p, LANES)), gspec((p, LANES)), gspec((hs, p)), gspec((hs, p)),
                  gspec((p, LANES)), gspec((p, LANES)), gspec((1, cw))],
        out_specs=gspec((rows, cw)),
        scratch_shapes=[pltpu.VMEM((cw, cw), BF16)],
        compiler_params=pltpu.CompilerParams(
            dimension_semantics=("parallel",), vmem_limit_bytes=VMEM_LIMIT),
        name="s5",
    )(u3, col(lam_re), col(lam_im), row(lam_re), row(lam_im), lstep,
      b_c(b_re), b_c(b_im), b_r(b_re), b_r(b_im), c_c(c_re), c_c(c_im), d_t)


def _merge_kernel(x_ref, gate_ref, ya_ref, ys_ref, sg0_ref, sg1_ref, sg2_ref, yc_ref,
                  gluw_ref, glub_ref, pa_ref, pb_ref, pc_ref, wout_ref, fnw_ref, o_ref, *, final):
    d = x_ref.shape[-1]
    ys = jax.nn.gelu(ys_ref[...])
    glu = jax.nn.sigmoid(_dot(ys.astype(BF16), gluw_ref[...]) + glub_ref[...])
    sgate = jnp.concatenate([sg0_ref[...], sg1_ref[...], sg2_ref[...]], axis=1)
    yb = ys * glu * _silu(sgate)
    pa = _dot(ya_ref[...].astype(BF16), pa_ref[...])
    pb = _dot(yb.astype(BF16), pb_ref[...])
    pc = _dot(yc_ref[...].astype(BF16), pc_ref[...])
    merged = (jax.nn.sigmoid(gate_ref[:, 0:d]) * pa
              + jax.nn.sigmoid(gate_ref[:, d:2 * d]) * pb
              + jax.nn.sigmoid(gate_ref[:, 2 * d:3 * d]) * pc)
    out = x_ref[...] + _dot(merged.astype(BF16), wout_ref[...])
    if final:
        out = out * lax.rsqrt(jnp.mean(out * out, axis=-1, keepdims=True) + NORM_EPS) * fnw_ref[...]
    o_ref[...] = out.astype(o_ref.dtype)


def _merge(x2d, proj2d, ya, ys, yc, glu_w, glu_b, proj_a, proj_b, proj_c, w_out, final_w, final, tm=256):
    t, d = x2d.shape
    tm = min(tm, t)
    sgw = S5_WIDTH // 3
    assert OFF_SG % sgw == 0 and sgw % LANES == 0

    def wspec(shape):
        return pl.BlockSpec(shape, lambda i: (0, 0), pipeline_mode=pl.Buffered(1))

    return pl.pallas_call(
        functools.partial(_merge_kernel, final=final),
        out_shape=jax.ShapeDtypeStruct((t, d), F32),
        grid=(t // tm,),
        in_specs=[
            pl.BlockSpec((tm, d), lambda i: (i, 0)),
            pl.BlockSpec((tm, 3 * d), lambda i: (i, OFF_MERGE // (3 * d))),
            pl.BlockSpec((tm, GDN_WIDTH), lambda i: (i, 0)),
            pl.BlockSpec((tm, S5_WIDTH), lambda i: (i, 0)),
            pl.BlockSpec((tm, sgw), lambda i: (i, OFF_SG // sgw)),
            pl.BlockSpec((tm, sgw), lambda i: (i, OFF_SG // sgw + 1)),
            pl.BlockSpec((tm, sgw), lambda i: (i, OFF_SG // sgw + 2)),
            pl.BlockSpec((tm, M2_WIDTH), lambda i: (i, 0)),
            wspec((S5_WIDTH, S5_WIDTH)), wspec((1, S5_WIDTH)),
            wspec((GDN_WIDTH, d)), wspec((S5_WIDTH, d)), wspec((M2_WIDTH, d)), wspec((d, d)),
            wspec((1, d)),
        ],
        out_specs=pl.BlockSpec((tm, d), lambda i: (i, 0)),
        compiler_params=pltpu.CompilerParams(
            dimension_semantics=("parallel",), vmem_limit_bytes=VMEM_LIMIT),
        name="merge",
    )(x2d, proj2d, ya, ys, proj2d, proj2d, proj2d, yc,
      glu_w.astype(BF16), glu_b.astype(F32).reshape(1, -1),
      proj_a.astype(BF16), proj_b.astype(BF16), proj_c.astype(BF16), w_out.astype(BF16),
      final_w.astype(F32).reshape(1, d))


def _pack_w_in(w):
    gw, sw, mw = GDN_WIDTH, S5_WIDTH, M2_WIDTH
    conv_dim = mw + 2 * M2_BC
    o = 0
    seg = {}
    for name, width in (("qkv", 3 * gw), ("az", gw), ("beta", GDN_HEADS), ("decay", GDN_HEADS),
                        ("su", sw), ("sg", sw), ("cz", mw), ("xbc", conv_dim), ("dt", M2_HEADS),
                        ("merge", 3 * w.shape[0])):
        seg[name] = w[:, o:o + width]
        o += width
    main = jnp.concatenate([seg["merge"], seg["qkv"], seg["xbc"], seg["az"], seg["cz"], seg["su"], seg["sg"]],
                           axis=1)
    pad = jnp.zeros((w.shape[0], LANES - 2 * GDN_HEADS - M2_HEADS), w.dtype)
    small = jnp.concatenate([seg["beta"], seg["decay"], seg["dt"], pad], axis=1)
    return main.astype(BF16), small.astype(BF16)


def _layer(x2d, bsz, seqlen, norm_w, w_in, gdn_conv_w, gdn_a_log, gdn_dt_bias, gdn_norm_w,
           s5_lam_re, s5_lam_im, s5_log_step, s5_b_re, s5_b_im, s5_c_re, s5_c_im, s5_d,
           s5_glu_w, s5_glu_b, m2_conv_w, m2_conv_b, m2_a_log, m2_dt_bias, m2_d, m2_norm_w,
           proj_a, proj_b, proj_c, w_out, final_w, final):
    t = bsz * seqlen
    nc = seqlen // CHUNK
    w_main, w_small = _pack_w_in(w_in)
    h = _rmsnorm(x2d, norm_w, BF16)
    proj = _matmul(h, w_main, 1024, 512)
    small = _matmul(h, w_small, 1024, LANES)
    proj3 = proj.reshape(bsz, seqlen, MAIN_W)
    small3 = small.reshape(bsz, seqlen, LANES)

    ya = _gdn(proj3, small3, gdn_conv_w, gdn_a_log, gdn_dt_bias, gdn_norm_w)
    yc = _ssd(proj3, small3, m2_conv_w, m2_conv_b, m2_a_log, m2_dt_bias, m2_d, m2_norm_w)

    u = proj[:, OFF_SU:OFF_SU + S5_WIDTH].reshape(bsz * nc, CHUNK, S5_GROUPS, S5_GROUP_SIZE)
    u = jnp.transpose(u, (2, 0, 1, 3)).reshape(S5_GROUPS, bsz * nc, CHUNK * S5_GROUP_SIZE)
    ys = _s5(u, s5_lam_re, s5_lam_im, s5_log_step, s5_b_re, s5_b_im, s5_c_re, s5_c_im, s5_d, nc)
    ys = ys.reshape(S5_GROUPS, bsz * nc, CHUNK, S5_GROUP_SIZE)
    ys = jnp.transpose(ys, (1, 2, 0, 3)).reshape(t, S5_WIDTH)

    return _merge(x2d, proj, ya.reshape(t, GDN_WIDTH), ys, yc.reshape(t, M2_WIDTH),
                  s5_glu_w, s5_glu_b, proj_a, proj_b, proj_c, w_out, final_w, final)


def kernel(x, norm_w, w_in, gdn_conv_w, gdn_a_log, gdn_dt_bias, gdn_norm_w, s5_lam_re, s5_lam_im, s5_log_step, s5_b_re, s5_b_im, s5_c_re, s5_c_im, s5_d, s5_glu_w, s5_glu_b, m2_conv_w, m2_conv_b, m2_a_log, m2_dt_bias, m2_d, m2_norm_w, proj_a, proj_b, proj_c, w_out, final_norm_w):
    bsz, seqlen, d = x.shape
    depth = norm_w.shape[0]
    x2d = x.reshape(bsz * seqlen, d)
    for i in range(depth):
        x2d = _layer(x2d, bsz, seqlen, norm_w[i], w_in[i], gdn_conv_w[i], gdn_a_log[i], gdn_dt_bias[i],
                     gdn_norm_w[i], s5_lam_re[i], s5_lam_im[i], s5_log_step[i], s5_b_re[i], s5_b_im[i],
                     s5_c_re[i], s5_c_im[i], s5_d[i], s5_glu_w[i], s5_glu_b[i],
                     m2_conv_w[i], m2_conv_b[i], m2_a_log[i], m2_dt_bias[i], m2_d[i], m2_norm_w[i],
                     proj_a[i], proj_b[i], proj_c[i], w_out[i], final_norm_w, i == depth - 1)
    return x2d.reshape(bsz, seqlen, d)
```

```python
import functools
import math

import jax
import jax.numpy as jnp
from jax import lax
from jax.experimental import pallas as pl
from jax.experimental.pallas import tpu as pltpu

F32 = jnp.float32
BF16 = jnp.bfloat16
HIGHEST = lax.Precision.HIGHEST

NORM_EPS = 1e-6
CHUNK = 64
CONV_K = 4
HALO = 8
LANES = 128

GDN_HEADS = 8
GDN_HEAD_DIM = 128
GDN_WIDTH = GDN_HEADS * GDN_HEAD_DIM
S5_GROUP_SIZE = 16
S5_GROUPS = 48
S5_STATE = 64
S5_WIDTH = S5_GROUPS * S5_GROUP_SIZE
M2_HEADS = 16
M2_HEAD_DIM = 64
M2_WIDTH = M2_HEADS * M2_HEAD_DIM
M2_GROUPS = 4
M2_STATE = 128
M2_BC = M2_GROUPS * M2_STATE
M2_GROUP_W = M2_WIDTH // M2_GROUPS

OFF_MERGE = 0
OFF_QKV = OFF_MERGE + 3 * 2048
OFF_XBC = OFF_QKV + 3 * GDN_WIDTH
OFF_AZ = OFF_XBC + M2_WIDTH + 2 * M2_BC
OFF_CZ = OFF_AZ + GDN_WIDTH
OFF_SU = OFF_CZ + M2_WIDTH
OFF_SG = OFF_SU + S5_WIDTH
MAIN_W = OFF_SG + S5_WIDTH
LANE_BETA = 0
LANE_DECAY = GDN_HEADS
LANE_DT = 2 * GDN_HEADS

VMEM_LIMIT = 56 * 1024 * 1024


def _softplus(x):
    return jnp.maximum(x, 0.0) + jnp.log(1.0 + jnp.exp(-jnp.abs(x)))


def _silu(x):
    return x * jax.nn.sigmoid(x)


def _dot(a, b):
    return jnp.dot(a, b, preferred_element_type=F32)


def _dot_nt(a, b):
    return lax.dot_general(a, b, (((1,), (1,)), ((), ())), preferred_element_type=F32)


def _dot_hi(a, b):
    return jnp.dot(a, b, preferred_element_type=F32, precision=HIGHEST)


def _dot_b(a, b):
    return _dot(a.astype(BF16), b.astype(BF16))


def _tri(n, strict=False):
    r = lax.broadcasted_iota(jnp.int32, (n, n), 0)
    c = lax.broadcasted_iota(jnp.int32, (n, n), 1)
    return (r > c) if strict else (r >= c)


def _rmsnorm_kernel(x_ref, w_ref, o_ref):
    x = x_ref[...]
    y = x * lax.rsqrt(jnp.mean(x * x, axis=-1, keepdims=True) + NORM_EPS)
    o_ref[...] = (y * w_ref[...]).astype(o_ref.dtype)


def _rmsnorm(x2d, w, out_dtype, tm=512):
    t, d = x2d.shape
    tm = min(tm, t)
    return pl.pallas_call(
        _rmsnorm_kernel,
        out_shape=jax.ShapeDtypeStruct((t, d), out_dtype),
        grid=(t // tm,),
        in_specs=[pl.BlockSpec((tm, d), lambda i: (i, 0)),
                  pl.BlockSpec((1, d), lambda i: (0, 0))],
        out_specs=pl.BlockSpec((tm, d), lambda i: (i, 0)),
        compiler_params=pltpu.CompilerParams(dimension_semantics=("parallel",)),
        name="rmsnorm",
    )(x2d, w.reshape(1, d))


def _matmul_kernel(a_ref, b_ref, o_ref):
    o_ref[...] = _dot(a_ref[...], b_ref[...]).astype(o_ref.dtype)


def _matmul(a, b, tm, tn, out_dtype=F32):
    m, k = a.shape
    _, n = b.shape
    tm = min(tm, m)
    tn = min(tn, n)
    return pl.pallas_call(
        _matmul_kernel,
        out_shape=jax.ShapeDtypeStruct((m, n), out_dtype),
        grid=(m // tm, n // tn),
        in_specs=[pl.BlockSpec((tm, k), lambda i, j: (i, 0)),
                  pl.BlockSpec((k, tn), lambda i, j: (0, j))],
        out_specs=pl.BlockSpec((tm, tn), lambda i, j: (i, j)),
        compiler_params=pltpu.CompilerParams(
            dimension_semantics=("parallel", "parallel"), vmem_limit_bytes=VMEM_LIMIT),
        name="inproj",
    )(a, b)


def _fill_ext(ext_ref, prev_ref, cur_ref, first):
    prev = prev_ref[...]
    ext_ref[0:HALO, :] = jnp.where(first, jnp.zeros_like(prev), prev)
    ext_ref[HALO:HALO + CHUNK, :] = cur_ref[...]


def _conv_tile(ext_ref, w_ref, col, width):
    acc = None
    for k in range(CONV_K):
        start = HALO - (CONV_K - 1) + k
        term = w_ref[k:k + 1, col:col + width] * ext_ref[start:start + CHUNK, col:col + width]
        acc = term if acc is None else acc + term
    return acc


def _gdn_kernel(q_ref, qprev_ref, z_ref, sm_ref, convw_ref, alog_ref, dtb_ref, normw_ref,
                o_ref, state_ref, ext_ref):
    c = pl.program_id(1)
    first = c == 0

    @pl.when(first)
    def _():
        state_ref[...] = jnp.zeros_like(state_ref)

    _fill_ext(ext_ref, qprev_ref, q_ref, first)

    sm = sm_ref[...]
    beta_all = jax.nn.sigmoid(sm)
    g_all = -jnp.exp(alog_ref[...]) * _softplus(sm + dtb_ref[...])
    causal = _tri(CHUNK)
    strict = _tri(CHUNK, strict=True)
    ltri = causal.astype(F32)
    gc_all = _dot_hi(ltri, g_all)
    gc_t = gc_all.T
    eg_all = jnp.exp(gc_all)
    glast_all = gc_all[CHUNK - 1:CHUNK, :]
    etail_all = jnp.exp(glast_all - gc_all)
    eglast_all = jnp.exp(glast_all)
    eye = (lax.broadcasted_iota(jnp.int32, (CHUNK, CHUNK), 0)
           == lax.broadcasted_iota(jnp.int32, (CHUNK, CHUNK), 1)).astype(F32)
    scale = GDN_HEAD_DIM ** -0.5

    heads = range(GDN_HEADS)
    dk = GDN_HEAD_DIM

    def prep(h):
        lo = h * dk
        q = _silu(_conv_tile(ext_ref, convw_ref, lo, dk))
        k = _silu(_conv_tile(ext_ref, convw_ref, GDN_WIDTH + lo, dk))
        v = _silu(_conv_tile(ext_ref, convw_ref, 2 * GDN_WIDTH + lo, dk))
        q = q * lax.rsqrt(jnp.sum(q * q, axis=-1, keepdims=True) + NORM_EPS) * scale
        k = k * lax.rsqrt(jnp.sum(k * k, axis=-1, keepdims=True) + NORM_EPS)
        beta = beta_all[:, LANE_BETA + h:LANE_BETA + h + 1]
        ld = LANE_DECAY + h
        eg = eg_all[:, ld:ld + 1]
        decay = jnp.exp(jnp.where(causal, gc_all[:, ld:ld + 1] - gc_t[ld:ld + 1, :], -jnp.inf))
        kb = k * beta
        return dict(
            lhs_kk=jnp.concatenate([kb, q], axis=0).astype(BF16), k_b=k.astype(BF16), decay=decay,
            rhs_uw=jnp.concatenate([v * beta, kb * eg], axis=1).astype(BF16),
            q_dec=(q * eg).astype(BF16),
            k_tail_t=(k * etail_all[:, ld:ld + 1]).T.astype(BF16),
            eglast=eglast_all[:, ld:ld + 1])

    hd = [prep(h) for h in heads]
    kk = [_dot_nt(d["lhs_kk"], d["k_b"]) for d in hd]
    qk = [(kk[h][CHUNK:] * hd[h]["decay"]).astype(BF16) for h in heads]

    b_pow = [-jnp.where(strict, kk[h][:CHUNK] * hd[h]["decay"], 0.0) for h in heads]
    x_inv = [eye + b for b in b_pow]
    b_pow = [_dot_b(b, b) for b in b_pow]
    n_sq = int(math.log2(CHUNK)) - 1
    for it in range(n_sq):
        if it < n_sq - 1:
            p = [_dot_b(jnp.concatenate([b_pow[h], x_inv[h]], axis=0), b_pow[h]) for h in heads]
            b_pow = [p[h][:CHUNK] for h in heads]
            x_inv = [x_inv[h] + p[h][CHUNK:] for h in heads]
        else:
            x_inv = [x_inv[h] + _dot_b(x_inv[h], b_pow[h]) for h in heads]

    uw = [_dot(x_inv[h].astype(BF16), hd[h]["rhs_uw"]) for h in heads]
    s_old = [state_ref[h] for h in heads]
    ws_qs = [_dot(jnp.concatenate([uw[h][:, dk:].astype(BF16), hd[h]["q_dec"]], axis=0),
                  s_old[h].astype(BF16)) for h in heads]
    v_new = [(uw[h][:, :dk] - ws_qs[h][:CHUNK]).astype(BF16) for h in heads]
    out = [ws_qs[h][CHUNK:] + _dot(qk[h], v_new[h]) for h in heads]
    for h in heads:
        state_ref[h] = s_old[h] * hd[h]["eglast"] + _dot(hd[h]["k_tail_t"], v_new[h])
    for h in heads:
        lo = h * dk
        o = out[h] * lax.rsqrt(jnp.mean(out[h] * out[h], axis=-1, keepdims=True) + NORM_EPS) * normw_ref[...]
        o_ref[:, lo:lo + dk] = (o * _silu(z_ref[:, lo:lo + dk])).astype(o_ref.dtype)


def _lane_vec(values, lane0):
    v = jnp.zeros((1, LANES), F32)
    return lax.dynamic_update_slice(v, values.astype(F32).reshape(1, -1), (0, lane0))


def _gdn(proj, small, conv_w, a_log, dt_bias, norm_w):
    bsz, seqlen, _ = proj.shape
    nc = seqlen // CHUNK
    wq = 3 * GDN_WIDTH
    rows_per_halo = CHUNK // HALO
    return pl.pallas_call(
        _gdn_kernel,
        out_shape=jax.ShapeDtypeStruct((bsz, seqlen, GDN_WIDTH), F32),
        grid=(bsz, nc),
        in_specs=[
            pl.BlockSpec((None, CHUNK, wq), lambda b, c: (b, c, OFF_QKV // wq)),
            pl.BlockSpec((None, HALO, wq),
                         lambda b, c: (b, jnp.maximum(c * rows_per_halo - 1, 0), OFF_QKV // wq)),
            pl.BlockSpec((None, CHUNK, GDN_WIDTH), lambda b, c: (b, c, OFF_AZ // GDN_WIDTH)),
            pl.BlockSpec((None, CHUNK, LANES), lambda b, c: (b, c, 0)),
            pl.BlockSpec((CONV_K, wq), lambda b, c: (0, 0)),
            pl.BlockSpec((1, LANES), lambda b, c: (0, 0)),
            pl.BlockSpec((1, LANES), lambda b, c: (0, 0)),
            pl.BlockSpec((1, GDN_HEAD_DIM), lambda b, c: (0, 0)),
        ],
        out_specs=pl.BlockSpec((None, CHUNK, GDN_WIDTH), lambda b, c: (b, c, 0)),
        scratch_shapes=[pltpu.VMEM((GDN_HEADS, GDN_HEAD_DIM, GDN_HEAD_DIM), F32),
                        pltpu.VMEM((HALO + CHUNK, wq), F32)],
        compiler_params=pltpu.CompilerParams(
            dimension_semantics=("parallel", "arbitrary"), vmem_limit_bytes=VMEM_LIMIT),
        name="gdn",
    )(proj, proj, proj, small, conv_w.astype(F32), _lane_vec(a_log, LANE_DECAY),
      _lane_vec(dt_bias, LANE_DECAY), norm_w.astype(F32).reshape(1, GDN_HEAD_DIM))


def _ssd_kernel(x_ref, bc_ref, xprev_ref, bcprev_ref, z_ref, sm_ref, convw_ref, convb_ref,
                alog_ref, dtb_ref, dskip_ref, normw_ref, expand_ref,
                o_ref, state_ref, ext_ref):
    c = pl.program_id(1)
    first = c == 0

    @pl.when(first)
    def _():
        state_ref[...] = jnp.zeros_like(state_ref)

    prev_x = xprev_ref[...]
    prev_bc = bcprev_ref[...]
    ext_ref[0:HALO, 0:M2_WIDTH] = jnp.where(first, jnp.zeros_like(prev_x), prev_x)
    ext_ref[0:HALO, M2_WIDTH:] = jnp.where(first, jnp.zeros_like(prev_bc), prev_bc)
    ext_ref[HALO:, 0:M2_WIDTH] = x_ref[...]
    ext_ref[HALO:, M2_WIDTH:] = bc_ref[...]

    sm = sm_ref[...]
    dt_all = _softplus(sm + dtb_ref[...])
    a_all = -jnp.exp(alog_ref[...]) * dt_all
    causal = _tri(CHUNK)
    acum = _dot_hi(causal.astype(F32), a_all)
    acum_t = acum.T
    ea = jnp.exp(acum)
    alast = acum[CHUNK - 1:CHUNK, :]
    ds = jnp.exp(alast - acum)
    spread = _dot_hi(jnp.concatenate([dt_all, ea, ds], axis=0), expand_ref[...].astype(F32))
    dt_x = spread[:CHUNK]
    ea_x = spread[CHUNK:2 * CHUNK]
    ds_x = spread[2 * CHUNK:]
    cdecay_x = ea_x[CHUNK - 1:CHUNK, :]

    lane = lax.broadcasted_iota(jnp.int32, (CHUNK, 2 * M2_HEAD_DIM), 1)
    heads_per_group = M2_HEADS // M2_GROUPS

    groups = range(M2_GROUPS)

    def prep(g):
        glo = g * M2_GROUP_W
        bcol = M2_WIDTH + g * M2_STATE
        ccol = M2_WIDTH + M2_BC + g * M2_STATE
        bm = _silu(_conv_tile(ext_ref, convw_ref, bcol, M2_STATE) + convb_ref[:, bcol:bcol + M2_STATE])
        cm = _silu(_conv_tile(ext_ref, convw_ref, ccol, M2_STATE) + convb_ref[:, ccol:ccol + M2_STATE])
        xs = _silu(_conv_tile(ext_ref, convw_ref, glo, M2_GROUP_W) + convb_ref[:, glo:glo + M2_GROUP_W])
        xdt = xs * dt_x[:, glo:glo + M2_GROUP_W]
        return dict(xs=xs, xdt_b=xdt.astype(BF16), xds=(xdt * ds_x[:, glo:glo + M2_GROUP_W]).astype(BF16),
                    bm_b=bm.astype(BF16), bm_t=bm.T.astype(BF16), cm_b=cm.astype(BF16))

    gd = [prep(g) for g in groups]
    scores = [_dot_nt(d["cm_b"], d["bm_b"]) for d in gd]
    s_prev = [state_ref[g] for g in groups]
    y_off = [_dot(gd[g]["cm_b"], s_prev[g].astype(BF16)) for g in groups]
    states = [_dot(d["bm_t"], d["xds"]) for d in gd]
    for g in groups:
        glo = g * M2_GROUP_W
        state_ref[g] = s_prev[g] * cdecay_x[:, glo:glo + M2_GROUP_W] + states[g]

    def diag(g, j, r):
        la = LANE_DT + g * heads_per_group + 2 * j + r
        seg = jnp.exp(jnp.where(causal, acum[:, la:la + 1] - acum_t[la:la + 1, :], -jnp.inf))
        x2 = gd[g]["xdt_b"][:, j * 2 * M2_HEAD_DIM:(j + 1) * 2 * M2_HEAD_DIM]
        return _dot((scores[g] * seg).astype(BF16), x2)

    pairs = range(heads_per_group // 2)
    yd = [[[diag(g, j, r) for r in range(2)] for j in pairs] for g in groups]
    for g in groups:
        glo = g * M2_GROUP_W
        parts = [jnp.where(lane < M2_HEAD_DIM, yd[g][j][0], yd[g][j][1]) for j in pairs]
        y = jnp.concatenate(parts, axis=1) + y_off[g] * ea_x[:, glo:glo + M2_GROUP_W]
        y = y + dskip_ref[:, glo:glo + M2_GROUP_W] * gd[g]["xs"]
        y = y * _silu(z_ref[:, glo:glo + M2_GROUP_W])
        y = y * lax.rsqrt(jnp.mean(y * y, axis=-1, keepdims=True) + NORM_EPS)
        o_ref[:, glo:glo + M2_GROUP_W] = (y * normw_ref[:, glo:glo + M2_GROUP_W]).astype(o_ref.dtype)


def _ssd(proj, small, conv_w, conv_b, a_log, dt_bias, d_skip, norm_w):
    bsz, seqlen, _ = proj.shape
    nc = seqlen // CHUNK
    rows_per_halo = CHUNK // HALO
    conv_dim = M2_WIDTH + 2 * M2_BC
    head_of_col = jnp.arange(M2_WIDTH) // M2_HEAD_DIM
    expand = (jnp.arange(LANES)[:, None] == (LANE_DT + head_of_col)[None, :]).astype(BF16)
    dskip_x = jnp.repeat(d_skip.astype(F32), M2_HEAD_DIM).reshape(1, M2_WIDTH)
    xi = OFF_XBC // M2_WIDTH

    def halo_idx(b, c):
        return jnp.maximum(c * rows_per_halo - 1, 0)

    return pl.pallas_call(
        _ssd_kernel,
        out_shape=jax.ShapeDtypeStruct((bsz, seqlen, M2_WIDTH), F32),
        grid=(bsz, nc),
        in_specs=[
            pl.BlockSpec((None, CHUNK, M2_WIDTH), lambda b, c: (b, c, xi)),
            pl.BlockSpec((None, CHUNK, 2 * M2_BC), lambda b, c: (b, c, xi + 1)),
            pl.BlockSpec((None, HALO, M2_WIDTH), lambda b, c: (b, halo_idx(b, c), xi)),
            pl.BlockSpec((None, HALO, 2 * M2_BC), lambda b, c: (b, halo_idx(b, c), xi + 1)),
            pl.BlockSpec((None, CHUNK, M2_WIDTH), lambda b, c: (b, c, OFF_CZ // M2_WIDTH)),
            pl.BlockSpec((None, CHUNK, LANES), lambda b, c: (b, c, 0)),
            pl.BlockSpec((CONV_K, conv_dim), lambda b, c: (0, 0)),
            pl.BlockSpec((1, conv_dim), lambda b, c: (0, 0)),
            pl.BlockSpec((1, LANES), lambda b, c: (0, 0)),
            pl.BlockSpec((1, LANES), lambda b, c: (0, 0)),
            pl.BlockSpec((1, M2_WIDTH), lambda b, c: (0, 0)),
            pl.BlockSpec((1, M2_WIDTH), lambda b, c: (0, 0)),
            pl.BlockSpec((LANES, M2_WIDTH), lambda b, c: (0, 0)),
        ],
        out_specs=pl.BlockSpec((None, CHUNK, M2_WIDTH), lambda b, c: (b, c, 0)),
        scratch_shapes=[pltpu.VMEM((M2_GROUPS, M2_STATE, M2_GROUP_W), F32),
                        pltpu.VMEM((HALO + CHUNK, conv_dim), F32)],
        compiler_params=pltpu.CompilerParams(
            dimension_semantics=("parallel", "arbitrary"), vmem_limit_bytes=VMEM_LIMIT),
        name="ssd",
    )(proj, proj, proj, proj, proj, small, conv_w.astype(F32), conv_b.astype(F32).reshape(1, conv_dim),
      _lane_vec(a_log, LANE_DT), _lane_vec(dt_bias, LANE_DT), dskip_x,
      norm_w.astype(F32).reshape(1, M2_WIDTH), expand)


def _s5_kernel(u_ref, lamre_c_ref, lamim_c_ref, lamre_r_ref, lamim_r_ref, lstep_ref,
               bre_c_ref, bim_c_ref, bre_r_ref, bim_r_ref, cre_c_ref, cim_c_ref, d_ref,
               o_ref, toep_ref, *, chunks_per_seq):
    p = S5_STATE
    hs = S5_GROUP_SIZE
    cw = CHUNK * hs
    step = jnp.exp(lstep_ref[0:1, :])
    step_r = step[:, :p]

    def disc(lre, lim, st):
        lre = jnp.minimum(lre, -1e-4)
        mag = jnp.exp(lre * st)
        are = mag * jnp.cos(lim * st)
        aim = mag * jnp.sin(lim * st)
        den = lre * lre + lim * lim
        fre = ((are - 1.0) * lre + aim * lim) / den
        fim = (aim * lre - (are - 1.0) * lim) / den
        return lre, are, aim, fre, fim

    lre_c, are_c, aim_c, fre_c, fim_c = disc(lamre_c_ref[...], lamim_c_ref[...], step)
    lim_c = lamim_c_ref[...]
    bbre_c = fre_c * bre_c_ref[...] - fim_c * bim_c_ref[...]
    bbim_c = fre_c * bim_c_ref[...] + fim_c * bre_c_ref[...]
    lre_r, _, _, fre_r, fim_r = disc(lamre_r_ref[0:1, :], lamim_r_ref[0:1, :], step_r)
    lim_r = lamim_r_ref[0:1, :]
    bbre_r = fre_r * bre_r_ref[...] - fim_r * bim_r_ref[...]
    bbim_r = fre_r * bim_r_ref[...] + fim_r * bre_r_ref[...]

    per = LANES // hs
    ntile = cw // LANES
    r_lane = lax.shift_right_logical(lax.broadcasted_iota(jnp.int32, (p, LANES), 1),
                                     int(math.log2(hs))).astype(F32)

    def cmul(a, b):
        return a[0] * b[0] - a[1] * b[1], a[0] * b[1] + a[1] * b[0]

    def cpow(expo):
        mag = jnp.exp(lre_c * step * expo)
        ang = lim_c * step * expo
        return mag * jnp.cos(ang), mag * jnp.sin(ang)

    a_c = (are_c, aim_c)
    c_fwd = cmul((cre_c_ref[...], cim_c_ref[...]), cpow(r_lane))
    c_fwd1 = cmul(c_fwd, a_c)
    b_rev = cmul((bbre_c, bbim_c), cpow(float(per - 1) - r_lane))
    a_per = cpow(float(per))
    tile_pow = [(jnp.ones_like(are_c), jnp.zeros_like(are_c))]
    for _ in range(ntile - 1):
        tile_pow.append(cmul(tile_pow[-1], a_per))
    m0 = [cmul(c_fwd, tile_pow[k]) for k in range(ntile)]
    m1 = [cmul(c_fwd1, tile_pow[k]) for k in range(ntile)]
    nn = [cmul(b_rev, tile_pow[ntile - 1 - k]) for k in range(ntile)]
    m0re = jnp.concatenate([t[0] for t in m0], axis=1)
    m0im = jnp.concatenate([t[1] for t in m0], axis=1)
    m1re = jnp.concatenate([t[0] for t in m1], axis=1)
    m1im = jnp.concatenate([t[1] for t in m1], axis=1)
    nre = jnp.concatenate([t[0] for t in nn], axis=1)
    nim = jnp.concatenate([t[1] for t in nn], axis=1)

    kt = _dot_hi(bbre_r, m0re) - _dot_hi(bbim_r, m0im)
    lane_k = lax.broadcasted_iota(jnp.int32, (hs, cw), 1)

    for r in range(per):
        kr = pltpu.roll(kt, r * hs, 1) if r else kt
        for q in range(ntile):
            s = q * per + r
            cut = cw - q * LANES
            rolled = kr if q == 0 else jnp.concatenate([kr[:, cut:], kr[:, :cut]], axis=1)
            toep_ref[s * hs:(s + 1) * hs, :] = jnp.where(lane_k >= s * hs, rolled, 0.0).astype(BF16)

    u = u_ref[...]
    ub = u.astype(BF16)
    y = _dot(ub, toep_ref[...])
    sre = _dot_nt(ub, nre.astype(BF16))
    sim = _dot_nt(ub, nim.astype(BF16))

    rows = u.shape[0]
    cidx = lax.broadcasted_iota(jnp.int32, (rows, p), 0) & (chunks_per_seq - 1)
    mag_c = jnp.exp(lre_r * step_r * float(CHUNK))
    pre = mag_c * jnp.cos(lim_r * step_r * float(CHUNK))
    pim = mag_c * jnp.sin(lim_r * step_r * float(CHUNK))
    d = 1
    while d < chunks_per_seq:
        ok = cidx >= d
        shre = jnp.where(ok, pltpu.roll(sre, d, 0), 0.0)
        shim = jnp.where(ok, pltpu.roll(sim, d, 0), 0.0)
        sre, sim = sre + pre * shre - pim * shim, sim + pre * shim + pim * shre
        pre, pim = pre * pre - pim * pim, 2.0 * pre * pim
        d *= 2
    ok = cidx >= 1
    xsre = jnp.where(ok, pltpu.roll(sre, 1, 0), 0.0)
    xsim = jnp.where(ok, pltpu.roll(sim, 1, 0), 0.0)
    y = y + _dot(xsre.astype(BF16), m1re.astype(BF16)) - _dot(xsim.astype(BF16), m1im.astype(BF16))
    o_ref[...] = (y + d_ref[...] * u).astype(o_ref.dtype)


def _s5(u3, lam_re, lam_im, log_step, b_re, b_im, c_re, c_im, d_skip, chunks_per_seq):
    g, rows, cw = u3.shape
    p, hs = S5_STATE, S5_GROUP_SIZE
    rep = LANES // hs
    f32 = lambda a: a.astype(F32)
    col = lambda a: jnp.broadcast_to(f32(a)[:, :, None], (g, p, LANES))
    row = lambda a: jnp.broadcast_to(f32(a)[:, None, :], (g, HALO, p))
    lstep = jnp.broadcast_to(f32(log_step)[:, None, None], (g, HALO, LANES))
    b_c = lambda a: jnp.tile(f32(a), (1, 1, rep))
    b_r = lambda a: jnp.swapaxes(f32(a), 1, 2)
    c_c = lambda a: jnp.tile(jnp.swapaxes(f32(a), 1, 2), (1, 1, rep))
    d_t = jnp.tile(f32(d_skip), (1, CHUNK)).reshape(g, 1, cw)

    def gspec(shape):
        return pl.BlockSpec((None,) + shape, lambda i: (i, 0, 0))

    return pl.pallas_call(
        functools.partial(_s5_kernel, chunks_per_seq=chunks_per_seq),
        out_shape=jax.ShapeDtypeStruct((g, rows, cw), F32),
        grid=(g,),
        in_specs=[gspec((rows, cw)),
                  gspec((p, LANES)), gspec((p, LANES)), gspec((HALO, p)), gspec((HALO, p)),
                  gspec((HALO, LANES)),
                  gspec((p, LANES)), gspec((p, LANES)), gspec((hs, p)), gspec((hs, p)),
                  gspec((p, LANES)), gspec((p, LANES)), gspec((1, cw))],
        out_specs=gspec((rows, cw)),
        scratch_shapes=[pltpu.VMEM((cw, cw), BF16)],
        compiler_params=pltpu.CompilerParams(
            dimension_semantics=("parallel",), vmem_limit_bytes=VMEM_LIMIT),
        name="s5",
    )(u3, col(lam_re), col(lam_im), row(lam_re), row(lam_im), lstep,
      b_c(b_re), b_c(b_im), b_r(b_re), b_r(b_im), c_c(c_re), c_c(c_im), d_t)


def _merge_kernel(x_ref, gate_ref, ya_ref, ys_ref, sg0_ref, sg1_ref, sg2_ref, yc_ref,
                  gluw_ref, glub_ref, pa_ref, pb_ref, pc_ref, wout_ref, nw_ref, o_ref, *h_ref, final):
    d = x_ref.shape[-1]
    ys = jax.nn.gelu(ys_ref[...])
    glu = jax.nn.sigmoid(_dot(ys.astype(BF16), gluw_ref[...]) + glub_ref[...])
    sgate = jnp.concatenate([sg0_ref[...], sg1_ref[...], sg2_ref[...]], axis=1)
    yb = ys * glu * _silu(sgate)
    pa = _dot(ya_ref[...].astype(BF16), pa_ref[...])
    pb = _dot(yb.astype(BF16), pb_ref[...])
    pc = _dot(yc_ref[...].astype(BF16), pc_ref[...])
    merged = (jax.nn.sigmoid(gate_ref[:, 0:d]) * pa
              + jax.nn.sigmoid(gate_ref[:, d:2 * d]) * pb
              + jax.nn.sigmoid(gate_ref[:, 2 * d:3 * d]) * pc)
    out = x_ref[...] + _dot(merged.astype(BF16), wout_ref[...])
    normed = out * lax.rsqrt(jnp.mean(out * out, axis=-1, keepdims=True) + NORM_EPS) * nw_ref[...]
    if final:
        o_ref[...] = normed.astype(o_ref.dtype)
    else:
        o_ref[...] = out.astype(o_ref.dtype)
        h_ref[0][...] = normed.astype(h_ref[0].dtype)


def _merge(x2d, proj2d, ya, ys, yc, glu_w, glu_b, proj_a, proj_b, proj_c, w_out, next_norm_w, final, tm=256):
    t, d = x2d.shape
    tm = min(tm, t)
    sgw = S5_WIDTH // 3
    assert OFF_SG % sgw == 0 and sgw % LANES == 0

    def wspec(shape):
        return pl.BlockSpec(shape, lambda i: (0, 0), pipeline_mode=pl.Buffered(1))

    row_spec = pl.BlockSpec((tm, d), lambda i: (i, 0))
    if final:
        out_shape, out_specs = jax.ShapeDtypeStruct((t, d), F32), row_spec
    else:
        out_shape = (jax.ShapeDtypeStruct((t, d), F32), jax.ShapeDtypeStruct((t, d), BF16))
        out_specs = (row_spec, row_spec)

    return pl.pallas_call(
        functools.partial(_merge_kernel, final=final),
        out_shape=out_shape,
        grid=(t // tm,),
        in_specs=[
            pl.BlockSpec((tm, d), lambda i: (i, 0)),
            pl.BlockSpec((tm, 3 * d), lambda i: (i, OFF_MERGE // (3 * d))),
            pl.BlockSpec((tm, GDN_WIDTH), lambda i: (i, 0)),
            pl.BlockSpec((tm, S5_WIDTH), lambda i: (i, 0)),
            pl.BlockSpec((tm, sgw), lambda i: (i, OFF_SG // sgw)),
            pl.BlockSpec((tm, sgw), lambda i: (i, OFF_SG // sgw + 1)),
            pl.BlockSpec((tm, sgw), lambda i: (i, OFF_SG // sgw + 2)),
            pl.BlockSpec((tm, M2_WIDTH), lambda i: (i, 0)),
            wspec((S5_WIDTH, S5_WIDTH)), wspec((1, S5_WIDTH)),
            wspec((GDN_WIDTH, d)), wspec((S5_WIDTH, d)), wspec((M2_WIDTH, d)), wspec((d, d)),
            wspec((1, d)),
        ],
        out_specs=out_specs,
        compiler_params=pltpu.CompilerParams(
            dimension_semantics=("parallel",), vmem_limit_bytes=VMEM_LIMIT),
        name="merge",
    )(x2d, proj2d, ya, ys, proj2d, proj2d, proj2d, yc,
      glu_w.astype(BF16), glu_b.astype(F32).reshape(1, -1),
      proj_a.astype(BF16), proj_b.astype(BF16), proj_c.astype(BF16), w_out.astype(BF16),
      next_norm_w.astype(F32).reshape(1, d))


def _w_in_segments(d_model):
    conv_dim = M2_WIDTH + 2 * M2_BC
    o = 0
    seg = {}
    for name, width in (("qkv", 3 * GDN_WIDTH), ("az", GDN_WIDTH), ("beta", GDN_HEADS), ("decay", GDN_HEADS),
                        ("su", S5_WIDTH), ("sg", S5_WIDTH), ("cz", M2_WIDTH), ("xbc", conv_dim),
                        ("dt", M2_HEADS), ("merge", 3 * d_model)):
        seg[name] = (o, width)
        o += width
    return seg


_MAIN_ORDER = (("merge", OFF_MERGE), ("qkv", OFF_QKV), ("xbc", OFF_XBC), ("az", OFF_AZ), ("cz", OFF_CZ),
               ("su", OFF_SU), ("sg", OFF_SG))


def _pack_kernel(w_ref, o_ref, *, seg):
    for name, dst in _MAIN_ORDER:
        src, width = seg[name]
        o_ref[:, dst:dst + width] = w_ref[:, src:src + width].astype(o_ref.dtype)


def _pack_w_in(w, tr=64):
    k, n = w.shape
    seg = _w_in_segments(k)
    main = pl.pallas_call(
        functools.partial(_pack_kernel, seg=seg),
        out_shape=jax.ShapeDtypeStruct((k, MAIN_W), BF16),
        grid=(k // tr,),
        in_specs=[pl.BlockSpec((tr, n), lambda i: (i, 0))],
        out_specs=pl.BlockSpec((tr, MAIN_W), lambda i: (i, 0)),
        compiler_params=pltpu.CompilerParams(dimension_semantics=("parallel",), vmem_limit_bytes=VMEM_LIMIT),
        name="packw",
    )(w)
    cols = [w[:, seg[name][0]:seg[name][0] + seg[name][1]] for name in ("beta", "decay", "dt")]
    pad = jnp.zeros((k, LANES - 2 * GDN_HEADS - M2_HEADS), w.dtype)
    small = jnp.concatenate(cols + [pad], axis=1)
    return main, small.astype(BF16)


def _layer(x2d, h, bsz, seqlen, w_in, gdn_conv_w, gdn_a_log, gdn_dt_bias, gdn_norm_w,
           s5_lam_re, s5_lam_im, s5_log_step, s5_b_re, s5_b_im, s5_c_re, s5_c_im, s5_d,
           s5_glu_w, s5_glu_b, m2_conv_w, m2_conv_b, m2_a_log, m2_dt_bias, m2_d, m2_norm_w,
           proj_a, proj_b, proj_c, w_out, next_norm_w, final):
    t = bsz * seqlen
    nc = seqlen // CHUNK
    w_main, w_small = _pack_w_in(w_in)
    proj = _matmul(h, w_main, 1024, 512)
    small = _matmul(h, w_small, 1024, LANES)
    proj3 = proj.reshape(bsz, seqlen, MAIN_W)
    small3 = small.reshape(bsz, seqlen, LANES)

    ya = _gdn(proj3, small3, gdn_conv_w, gdn_a_log, gdn_dt_bias, gdn_norm_w)
    yc = _ssd(proj3, small3, m2_conv_w, m2_conv_b, m2_a_log, m2_dt_bias, m2_d, m2_norm_w)

    u = proj[:, OFF_SU:OFF_SU + S5_WIDTH].reshape(bsz * nc, CHUNK, S5_GROUPS, S5_GROUP_SIZE)
    u = jnp.transpose(u, (2, 0, 1, 3)).reshape(S5_GROUPS, bsz * nc, CHUNK * S5_GROUP_SIZE)
    ys = _s5(u, s5_lam_re, s5_lam_im, s5_log_step, s5_b_re, s5_b_im, s5_c_re, s5_c_im, s5_d, nc)
    ys = ys.reshape(S5_GROUPS, bsz * nc, CHUNK, S5_GROUP_SIZE)
    ys = jnp.transpose(ys, (1, 2, 0, 3)).reshape(t, S5_WIDTH)

    return _merge(x2d, proj, ya.reshape(t, GDN_WIDTH), ys, yc.reshape(t, M2_WIDTH),
                  s5_glu_w, s5_glu_b, proj_a, proj_b, proj_c, w_out, next_norm_w, final)


def kernel(x, norm_w, w_in, gdn_conv_w, gdn_a_log, gdn_dt_bias, gdn_norm_w, s5_lam_re, s5_lam_im, s5_log_step, s5_b_re, s5_b_im, s5_c_re, s5_c_im, s5_d, s5_glu_w, s5_glu_b, m2_conv_w, m2_conv_b, m2_a_log, m2_dt_bias, m2_d, m2_norm_w, proj_a, proj_b, proj_c, w_out, final_norm_w):
    bsz, seqlen, d = x.shape
    depth = norm_w.shape[0]
    x2d = x.reshape(bsz * seqlen, d)
    h = _rmsnorm(x2d, norm_w[0], BF16)
    for i in range(depth):
        final = i == depth - 1
        res = _layer(x2d, h, bsz, seqlen, w_in[i], gdn_conv_w[i], gdn_a_log[i], gdn_dt_bias[i],
                     gdn_norm_w[i], s5_lam_re[i], s5_lam_im[i], s5_log_step[i], s5_b_re[i], s5_b_im[i],
                     s5_c_re[i], s5_c_im[i], s5_d[i], s5_glu_w[i], s5_glu_b[i],
                     m2_conv_w[i], m2_conv_b[i], m2_a_log[i], m2_dt_bias[i], m2_d[i], m2_norm_w[i],
                     proj_a[i], proj_b[i], proj_c[i], w_out[i],
                     final_norm_w if final else norm_w[i + 1], final)
        if final:
            return res.reshape(bsz, seqlen, d)
        x2d, h = res
```

```python
import functools
import math

import jax
import jax.numpy as jnp
from jax import lax
from jax.experimental import pallas as pl
from jax.experimental.pallas import tpu as pltpu

F32 = jnp.float32
BF16 = jnp.bfloat16
HIGHEST = lax.Precision.HIGHEST

NORM_EPS = 1e-6
CHUNK = 64
CONV_K = 4
HALO = 8
LANES = 128

GDN_HEADS = 8
GDN_HEAD_DIM = 128
GDN_WIDTH = GDN_HEADS * GDN_HEAD_DIM
S5_GROUP_SIZE = 16
S5_GROUPS = 48
S5_STATE = 64
S5_WIDTH = S5_GROUPS * S5_GROUP_SIZE
M2_HEADS = 16
M2_HEAD_DIM = 64
M2_WIDTH = M2_HEADS * M2_HEAD_DIM
M2_GROUPS = 4
M2_STATE = 128
M2_BC = M2_GROUPS * M2_STATE
M2_GROUP_W = M2_WIDTH // M2_GROUPS

OFF_MERGE = 0
OFF_QKV = OFF_MERGE + 3 * 2048
OFF_XBC = OFF_QKV + 3 * GDN_WIDTH
OFF_AZ = OFF_XBC + M2_WIDTH + 2 * M2_BC
OFF_CZ = OFF_AZ + GDN_WIDTH
OFF_SU = OFF_CZ + M2_WIDTH
OFF_SG = OFF_SU + S5_WIDTH
MAIN_W = OFF_SG + S5_WIDTH
LANE_BETA = 0
LANE_DECAY = GDN_HEADS
LANE_DT = 2 * GDN_HEADS

VMEM_LIMIT = 56 * 1024 * 1024
INPROJ_TM = 2048
INPROJ_TN = 512


def _softplus(x):
    return jnp.maximum(x, 0.0) + jnp.log(1.0 + jnp.exp(-jnp.abs(x)))


def _silu(x):
    return x * jax.nn.sigmoid(x)


def _dot(a, b):
    return jnp.dot(a, b, preferred_element_type=F32)


def _dot_nt(a, b):
    return lax.dot_general(a, b, (((1,), (1,)), ((), ())), preferred_element_type=F32)


def _dot_hi(a, b):
    return jnp.dot(a, b, preferred_element_type=F32, precision=HIGHEST)


def _dot_b(a, b):
    return _dot(a.astype(BF16), b.astype(BF16))


def _tri(n, strict=False):
    r = lax.broadcasted_iota(jnp.int32, (n, n), 0)
    c = lax.broadcasted_iota(jnp.int32, (n, n), 1)
    return (r > c) if strict else (r >= c)


def _rmsnorm_kernel(x_ref, w_ref, o_ref):
    x = x_ref[...]
    y = x * lax.rsqrt(jnp.mean(x * x, axis=-1, keepdims=True) + NORM_EPS)
    o_ref[...] = (y * w_ref[...]).astype(o_ref.dtype)


def _rmsnorm(x2d, w, out_dtype, tm=512):
    t, d = x2d.shape
    tm = min(tm, t)
    return pl.pallas_call(
        _rmsnorm_kernel,
        out_shape=jax.ShapeDtypeStruct((t, d), out_dtype),
        grid=(t // tm,),
        in_specs=[pl.BlockSpec((tm, d), lambda i: (i, 0)),
                  pl.BlockSpec((1, d), lambda i: (0, 0))],
        out_specs=pl.BlockSpec((tm, d), lambda i: (i, 0)),
        compiler_params=pltpu.CompilerParams(dimension_semantics=("parallel",)),
        name="rmsnorm",
    )(x2d, w.reshape(1, d))


def _matmul_kernel(a_ref, b_ref, o_ref):
    o_ref[...] = _dot(a_ref[...], b_ref[...]).astype(o_ref.dtype)


def _matmul(a, b, tm, tn, out_dtype=F32):
    m, k = a.shape
    _, n = b.shape
    tm = min(tm, m)
    tn = min(tn, n)
    return pl.pallas_call(
        _matmul_kernel,
        out_shape=jax.ShapeDtypeStruct((m, n), out_dtype),
        grid=(m // tm, n // tn),
        in_specs=[pl.BlockSpec((tm, k), lambda i, j: (i, 0)),
                  pl.BlockSpec((k, tn), lambda i, j: (0, j))],
        out_specs=pl.BlockSpec((tm, tn), lambda i, j: (i, j)),
        compiler_params=pltpu.CompilerParams(
            dimension_semantics=("parallel", "parallel"), vmem_limit_bytes=VMEM_LIMIT),
        name="inproj",
    )(a, b)


def _fill_ext(ext_ref, prev_ref, cur_ref, first):
    prev = prev_ref[...]
    ext_ref[0:HALO, :] = jnp.where(first, jnp.zeros_like(prev), prev)
    ext_ref[HALO:HALO + CHUNK, :] = cur_ref[...]


def _conv_tile(ext_ref, w_ref, col, width):
    acc = None
    for k in range(CONV_K):
        start = HALO - (CONV_K - 1) + k
        term = w_ref[k:k + 1, col:col + width] * ext_ref[start:start + CHUNK, col:col + width]
        acc = term if acc is None else acc + term
    return acc


def _gdn_kernel(q_ref, qprev_ref, z_ref, sm_ref, convw_ref, alog_ref, dtb_ref, normw_ref,
                o_ref, state_ref, ext_ref):
    c = pl.program_id(1)
    first = c == 0

    @pl.when(first)
    def _():
        state_ref[...] = jnp.zeros_like(state_ref)

    _fill_ext(ext_ref, qprev_ref, q_ref, first)

    sm = sm_ref[...]
    beta_all = jax.nn.sigmoid(sm)
    g_all = -jnp.exp(alog_ref[...]) * _softplus(sm + dtb_ref[...])
    causal = _tri(CHUNK)
    strict = _tri(CHUNK, strict=True)
    ltri = causal.astype(F32)
    gc_all = _dot_hi(ltri, g_all)
    gc_t = gc_all.T
    eg_all = jnp.exp(gc_all)
    glast_all = gc_all[CHUNK - 1:CHUNK, :]
    etail_all = jnp.exp(glast_all - gc_all)
    eglast_all = jnp.exp(glast_all)
    eye = (lax.broadcasted_iota(jnp.int32, (CHUNK, CHUNK), 0)
           == lax.broadcasted_iota(jnp.int32, (CHUNK, CHUNK), 1)).astype(F32)
    scale = GDN_HEAD_DIM ** -0.5

    heads = range(GDN_HEADS)
    dk = GDN_HEAD_DIM

    def prep(h):
        lo = h * dk
        q = _silu(_conv_tile(ext_ref, convw_ref, lo, dk))
        k = _silu(_conv_tile(ext_ref, convw_ref, GDN_WIDTH + lo, dk))
        v = _silu(_conv_tile(ext_ref, convw_ref, 2 * GDN_WIDTH + lo, dk))
        q = q * lax.rsqrt(jnp.sum(q * q, axis=-1, keepdims=True) + NORM_EPS) * scale
        k = k * lax.rsqrt(jnp.sum(k * k, axis=-1, keepdims=True) + NORM_EPS)
        beta = beta_all[:, LANE_BETA + h:LANE_BETA + h + 1]
        ld = LANE_DECAY + h
        eg = eg_all[:, ld:ld + 1]
        decay = jnp.exp(jnp.where(causal, gc_all[:, ld:ld + 1] - gc_t[ld:ld + 1, :], -jnp.inf))
        kb = k * beta
        return dict(
            lhs_kk=jnp.concatenate([kb, q], axis=0).astype(BF16), k_b=k.astype(BF16), decay=decay,
            rhs_uw=jnp.concatenate([v * beta, kb * eg], axis=1).astype(BF16),
            q_dec=(q * eg).astype(BF16),
            k_tail_t=(k * etail_all[:, ld:ld + 1]).T.astype(BF16),
            eglast=eglast_all[:, ld:ld + 1])

    hd = [prep(h) for h in heads]
    kk = [_dot_nt(d["lhs_kk"], d["k_b"]) for d in hd]
    qk = [(kk[h][CHUNK:] * hd[h]["decay"]).astype(BF16) for h in heads]

    b_pow = [-jnp.where(strict, kk[h][:CHUNK] * hd[h]["decay"], 0.0) for h in heads]
    x_inv = [eye + b for b in b_pow]
    b_pow = [_dot_b(b, b) for b in b_pow]
    n_sq = int(math.log2(CHUNK)) - 1
    for it in range(n_sq):
        if it < n_sq - 1:
            p = [_dot_b(jnp.concatenate([b_pow[h], x_inv[h]], axis=0), b_pow[h]) for h in heads]
            b_pow = [p[h][:CHUNK] for h in heads]
            x_inv = [x_inv[h] + p[h][CHUNK:] for h in heads]
        else:
            x_inv = [x_inv[h] + _dot_b(x_inv[h], b_pow[h]) for h in heads]

    uw = [_dot(x_inv[h].astype(BF16), hd[h]["rhs_uw"]) for h in heads]
    s_old = [state_ref[h] for h in heads]
    ws_qs = [_dot(jnp.concatenate([uw[h][:, dk:].astype(BF16), hd[h]["q_dec"]], axis=0),
                  s_old[h].astype(BF16)) for h in heads]
    v_new = [(uw[h][:, :dk] - ws_qs[h][:CHUNK]).astype(BF16) for h in heads]
    out = [ws_qs[h][CHUNK:] + _dot(qk[h], v_new[h]) for h in heads]
    for h in heads:
        state_ref[h] = s_old[h] * hd[h]["eglast"] + _dot(hd[h]["k_tail_t"], v_new[h])
    for h in heads:
        lo = h * dk
        o = out[h] * lax.rsqrt(jnp.mean(out[h] * out[h], axis=-1, keepdims=True) + NORM_EPS) * normw_ref[...]
        o_ref[:, lo:lo + dk] = (o * _silu(z_ref[:, lo:lo + dk])).astype(o_ref.dtype)


def _lane_vec(values, lane0):
    v = jnp.zeros((1, LANES), F32)
    return lax.dynamic_update_slice(v, values.astype(F32).reshape(1, -1), (0, lane0))


def _gdn(proj, small, conv_w, a_log, dt_bias, norm_w):
    bsz, seqlen, _ = proj.shape
    nc = seqlen // CHUNK
    wq = 3 * GDN_WIDTH
    rows_per_halo = CHUNK // HALO
    return pl.pallas_call(
        _gdn_kernel,
        out_shape=jax.ShapeDtypeStruct((bsz, seqlen, GDN_WIDTH), F32),
        grid=(bsz, nc),
        in_specs=[
            pl.BlockSpec((None, CHUNK, wq), lambda b, c: (b, c, OFF_QKV // wq)),
            pl.BlockSpec((None, HALO, wq),
                         lambda b, c: (b, jnp.maximum(c * rows_per_halo - 1, 0), OFF_QKV // wq)),
            pl.BlockSpec((None, CHUNK, GDN_WIDTH), lambda b, c: (b, c, OFF_AZ // GDN_WIDTH)),
            pl.BlockSpec((None, CHUNK, LANES), lambda b, c: (b, c, 0)),
            pl.BlockSpec((CONV_K, wq), lambda b, c: (0, 0)),
            pl.BlockSpec((1, LANES), lambda b, c: (0, 0)),
            pl.BlockSpec((1, LANES), lambda b, c: (0, 0)),
            pl.BlockSpec((1, GDN_HEAD_DIM), lambda b, c: (0, 0)),
        ],
        out_specs=pl.BlockSpec((None, CHUNK, GDN_WIDTH), lambda b, c: (b, c, 0)),
        scratch_shapes=[pltpu.VMEM((GDN_HEADS, GDN_HEAD_DIM, GDN_HEAD_DIM), F32),
                        pltpu.VMEM((HALO + CHUNK, wq), F32)],
        compiler_params=pltpu.CompilerParams(
            dimension_semantics=("parallel", "arbitrary"), vmem_limit_bytes=VMEM_LIMIT),
        name="gdn",
    )(proj, proj, proj, small, conv_w.astype(F32), _lane_vec(a_log, LANE_DECAY),
      _lane_vec(dt_bias, LANE_DECAY), norm_w.astype(F32).reshape(1, GDN_HEAD_DIM))


def _ssd_kernel(x_ref, bc_ref, xprev_ref, bcprev_ref, z_ref, sm_ref, convw_ref, convb_ref,
                alog_ref, dtb_ref, dskip_ref, normw_ref, expand_ref,
                o_ref, state_ref, ext_ref):
    c = pl.program_id(1)
    first = c == 0

    @pl.when(first)
    def _():
        state_ref[...] = jnp.zeros_like(state_ref)

    prev_x = xprev_ref[...]
    prev_bc = bcprev_ref[...]
    ext_ref[0:HALO, 0:M2_WIDTH] = jnp.where(first, jnp.zeros_like(prev_x), prev_x)
    ext_ref[0:HALO, M2_WIDTH:] = jnp.where(first, jnp.zeros_like(prev_bc), prev_bc)
    ext_ref[HALO:, 0:M2_WIDTH] = x_ref[...]
    ext_ref[HALO:, M2_WIDTH:] = bc_ref[...]

    sm = sm_ref[...]
    dt_all = _softplus(sm + dtb_ref[...])
    a_all = -jnp.exp(alog_ref[...]) * dt_all
    causal = _tri(CHUNK)
    acum = _dot_hi(causal.astype(F32), a_all)
    acum_t = acum.T
    ea = jnp.exp(acum)
    alast = acum[CHUNK - 1:CHUNK, :]
    ds = jnp.exp(alast - acum)
    spread = _dot_hi(jnp.concatenate([dt_all, ea, ds], axis=0), expand_ref[...].astype(F32))
    dt_x = spread[:CHUNK]
    ea_x = spread[CHUNK:2 * CHUNK]
    ds_x = spread[2 * CHUNK:]
    cdecay_x = ea_x[CHUNK - 1:CHUNK, :]

    lane = lax.broadcasted_iota(jnp.int32, (CHUNK, 2 * M2_HEAD_DIM), 1)
    heads_per_group = M2_HEADS // M2_GROUPS

    groups = range(M2_GROUPS)

    def prep(g):
        glo = g * M2_GROUP_W
        bcol = M2_WIDTH + g * M2_STATE
        ccol = M2_WIDTH + M2_BC + g * M2_STATE
        bm = _silu(_conv_tile(ext_ref, convw_ref, bcol, M2_STATE) + convb_ref[:, bcol:bcol + M2_STATE])
        cm = _silu(_conv_tile(ext_ref, convw_ref, ccol, M2_STATE) + convb_ref[:, ccol:ccol + M2_STATE])
        xs = _silu(_conv_tile(ext_ref, convw_ref, glo, M2_GROUP_W) + convb_ref[:, glo:glo + M2_GROUP_W])
        xdt = xs * dt_x[:, glo:glo + M2_GROUP_W]
        return dict(xs=xs, xdt_b=xdt.astype(BF16), xds=(xdt * ds_x[:, glo:glo + M2_GROUP_W]).astype(BF16),
                    bm_b=bm.astype(BF16), bm_t=bm.T.astype(BF16), cm_b=cm.astype(BF16))

    gd = [prep(g) for g in groups]
    scores = [_dot_nt(d["cm_b"], d["bm_b"]) for d in gd]
    s_prev = [state_ref[g] for g in groups]
    y_off = [_dot(gd[g]["cm_b"], s_prev[g].astype(BF16)) for g in groups]
    states = [_dot(d["bm_t"], d["xds"]) for d in gd]
    for g in groups:
        glo = g * M2_GROUP_W
        state_ref[g] = s_prev[g] * cdecay_x[:, glo:glo + M2_GROUP_W] + states[g]

    def diag(g, j, r):
        la = LANE_DT + g * heads_per_group + 2 * j + r
        seg = jnp.exp(jnp.where(causal, acum[:, la:la + 1] - acum_t[la:la + 1, :], -jnp.inf))
        x2 = gd[g]["xdt_b"][:, j * 2 * M2_HEAD_DIM:(j + 1) * 2 * M2_HEAD_DIM]
        return _dot((scores[g] * seg).astype(BF16), x2)

    pairs = range(heads_per_group // 2)
    yd = [[[diag(g, j, r) for r in range(2)] for j in pairs] for g in groups]
    for g in groups:
        glo = g * M2_GROUP_W
        parts = [jnp.where(lane < M2_HEAD_DIM, yd[g][j][0], yd[g][j][1]) for j in pairs]
        y = jnp.concatenate(parts, axis=1) + y_off[g] * ea_x[:, glo:glo + M2_GROUP_W]
        y = y + dskip_ref[:, glo:glo + M2_GROUP_W] * gd[g]["xs"]
        y = y * _silu(z_ref[:, glo:glo + M2_GROUP_W])
        y = y * lax.rsqrt(jnp.mean(y * y, axis=-1, keepdims=True) + NORM_EPS)
        o_ref[:, glo:glo + M2_GROUP_W] = (y * normw_ref[:, glo:glo + M2_GROUP_W]).astype(o_ref.dtype)


def _ssd(proj, small, conv_w, conv_b, a_log, dt_bias, d_skip, norm_w):
    bsz, seqlen, _ = proj.shape
    nc = seqlen // CHUNK
    rows_per_halo = CHUNK // HALO
    conv_dim = M2_WIDTH + 2 * M2_BC
    head_of_col = jnp.arange(M2_WIDTH) // M2_HEAD_DIM
    expand = (jnp.arange(LANES)[:, None] == (LANE_DT + head_of_col)[None, :]).astype(BF16)
    dskip_x = jnp.repeat(d_skip.astype(F32), M2_HEAD_DIM).reshape(1, M2_WIDTH)
    xi = OFF_XBC // M2_WIDTH

    def halo_idx(b, c):
        return jnp.maximum(c * rows_per_halo - 1, 0)

    return pl.pallas_call(
        _ssd_kernel,
        out_shape=jax.ShapeDtypeStruct((bsz, seqlen, M2_WIDTH), F32),
        grid=(bsz, nc),
        in_specs=[
            pl.BlockSpec((None, CHUNK, M2_WIDTH), lambda b, c: (b, c, xi)),
            pl.BlockSpec((None, CHUNK, 2 * M2_BC), lambda b, c: (b, c, xi + 1)),
            pl.BlockSpec((None, HALO, M2_WIDTH), lambda b, c: (b, halo_idx(b, c), xi)),
            pl.BlockSpec((None, HALO, 2 * M2_BC), lambda b, c: (b, halo_idx(b, c), xi + 1)),
            pl.BlockSpec((None, CHUNK, M2_WIDTH), lambda b, c: (b, c, OFF_CZ // M2_WIDTH)),
            pl.BlockSpec((None, CHUNK, LANES), lambda b, c: (b, c, 0)),
            pl.BlockSpec((CONV_K, conv_dim), lambda b, c: (0, 0)),
            pl.BlockSpec((1, conv_dim), lambda b, c: (0, 0)),
            pl.BlockSpec((1, LANES), lambda b, c: (0, 0)),
            pl.BlockSpec((1, LANES), lambda b, c: (0, 0)),
            pl.BlockSpec((1, M2_WIDTH), lambda b, c: (0, 0)),
            pl.BlockSpec((1, M2_WIDTH), lambda b, c: (0, 0)),
            pl.BlockSpec((LANES, M2_WIDTH), lambda b, c: (0, 0)),
        ],
        out_specs=pl.BlockSpec((None, CHUNK, M2_WIDTH), lambda b, c: (b, c, 0)),
        scratch_shapes=[pltpu.VMEM((M2_GROUPS, M2_STATE, M2_GROUP_W), F32),
                        pltpu.VMEM((HALO + CHUNK, conv_dim), F32)],
        compiler_params=pltpu.CompilerParams(
            dimension_semantics=("parallel", "arbitrary"), vmem_limit_bytes=VMEM_LIMIT),
        name="ssd",
    )(proj, proj, proj, proj, proj, small, conv_w.astype(F32), conv_b.astype(F32).reshape(1, conv_dim),
      _lane_vec(a_log, LANE_DT), _lane_vec(dt_bias, LANE_DT), dskip_x,
      norm_w.astype(F32).reshape(1, M2_WIDTH), expand)


def _s5_kernel(u_ref, lamre_c_ref, lamim_c_ref, lamre_r_ref, lamim_r_ref, lstep_ref,
               bre_c_ref, bim_c_ref, bre_r_ref, bim_r_ref, cre_c_ref, cim_c_ref,
               o_ref, toep_ref, *, chunks_per_seq):
    p = S5_STATE
    hs = S5_GROUP_SIZE
    cw = CHUNK * hs
    step = jnp.exp(lstep_ref[0:1, :])
    step_r = step[:, :p]

    def disc(lre, lim, st):
        lre = jnp.minimum(lre, -1e-4)
        mag = jnp.exp(lre * st)
        are = mag * jnp.cos(lim * st)
        aim = mag * jnp.sin(lim * st)
        den = lre * lre + lim * lim
        fre = ((are - 1.0) * lre + aim * lim) / den
        fim = (aim * lre - (are - 1.0) * lim) / den
        return lre, are, aim, fre, fim

    lre_c, are_c, aim_c, fre_c, fim_c = disc(lamre_c_ref[...], lamim_c_ref[...], step)
    lim_c = lamim_c_ref[...]
    bbre_c = fre_c * bre_c_ref[...] - fim_c * bim_c_ref[...]
    bbim_c = fre_c * bim_c_ref[...] + fim_c * bre_c_ref[...]
    lre_r, _, _, fre_r, fim_r = disc(lamre_r_ref[0:1, :], lamim_r_ref[0:1, :], step_r)
    lim_r = lamim_r_ref[0:1, :]
    bbre_r = fre_r * bre_r_ref[...] - fim_r * bim_r_ref[...]
    bbim_r = fre_r * bim_r_ref[...] + fim_r * bre_r_ref[...]

    per = LANES // hs
    ntile = cw // LANES
    r_lane = lax.shift_right_logical(lax.broadcasted_iota(jnp.int32, (p, LANES), 1),
                                     int(math.log2(hs))).astype(F32)

    def cmul(a, b):
        return a[0] * b[0] - a[1] * b[1], a[0] * b[1] + a[1] * b[0]

    def cpow(expo):
        mag = jnp.exp(lre_c * step * expo)
        ang = lim_c * step * expo
        return mag * jnp.cos(ang), mag * jnp.sin(ang)

    a_c = (are_c, aim_c)
    c_fwd = cmul((cre_c_ref[...], cim_c_ref[...]), cpow(r_lane))
    c_fwd1 = cmul(c_fwd, a_c)
    b_rev = cmul((bbre_c, bbim_c), cpow(float(per - 1) - r_lane))
    a_per = cpow(float(per))
    tile_pow = [(jnp.ones_like(are_c), jnp.zeros_like(are_c))]
    for _ in range(ntile - 1):
        tile_pow.append(cmul(tile_pow[-1], a_per))
    m0 = [cmul(c_fwd, tile_pow[k]) for k in range(ntile)]
    m1 = [cmul(c_fwd1, tile_pow[k]) for k in range(ntile)]
    nn = [cmul(b_rev, tile_pow[ntile - 1 - k]) for k in range(ntile)]
    m0re = jnp.concatenate([t[0] for t in m0], axis=1)
    m0im = jnp.concatenate([t[1] for t in m0], axis=1)
    m1re = jnp.concatenate([t[0] for t in m1], axis=1)
    m1im = jnp.concatenate([t[1] for t in m1], axis=1)
    nre = jnp.concatenate([t[0] for t in nn], axis=1)
    nim = jnp.concatenate([t[1] for t in nn], axis=1)

    kt = _dot_hi(bbre_r, m0re) - _dot_hi(bbim_r, m0im)
    lane_k = lax.broadcasted_iota(jnp.int32, (hs, cw), 1)

    for r in range(per):
        kr = pltpu.roll(kt, r * hs, 1) if r else kt
        for q in range(ntile):
            s = q * per + r
            cut = cw - q * LANES
            rolled = kr if q == 0 else jnp.concatenate([kr[:, cut:], kr[:, :cut]], axis=1)
            toep_ref[s * hs:(s + 1) * hs, :] = jnp.where(lane_k >= s * hs, rolled, 0.0).astype(BF16)

    ub = u_ref[...]
    y = _dot(ub, toep_ref[...])
    sre = _dot_nt(ub, nre.astype(BF16))
    sim = _dot_nt(ub, nim.astype(BF16))

    rows = ub.shape[0]
    cidx = lax.broadcasted_iota(jnp.int32, (rows, p), 0) & (chunks_per_seq - 1)
    mag_c = jnp.exp(lre_r * step_r * float(CHUNK))
    pre = mag_c * jnp.cos(lim_r * step_r * float(CHUNK))
    pim = mag_c * jnp.sin(lim_r * step_r * float(CHUNK))
    d = 1
    while d < chunks_per_seq:
        ok = cidx >= d
        shre = jnp.where(ok, pltpu.roll(sre, d, 0), 0.0)
        shim = jnp.where(ok, pltpu.roll(sim, d, 0), 0.0)
        sre, sim = sre + pre * shre - pim * shim, sim + pre * shim + pim * shre
        pre, pim = pre * pre - pim * pim, 2.0 * pre * pim
        d *= 2
    ok = cidx >= 1
    xsre = jnp.where(ok, pltpu.roll(sre, 1, 0), 0.0)
    xsim = jnp.where(ok, pltpu.roll(sim, 1, 0), 0.0)
    y = y + _dot(xsre.astype(BF16), m1re.astype(BF16)) - _dot(xsim.astype(BF16), m1im.astype(BF16))
    o_ref[...] = y.astype(o_ref.dtype)


def _s5(u3, lam_re, lam_im, log_step, b_re, b_im, c_re, c_im, chunks_per_seq):
    g, rows, cw = u3.shape
    p, hs = S5_STATE, S5_GROUP_SIZE
    rep = LANES // hs
    f32 = lambda a: a.astype(F32)
    col = lambda a: jnp.broadcast_to(f32(a)[:, :, None], (g, p, LANES))
    row = lambda a: jnp.broadcast_to(f32(a)[:, None, :], (g, HALO, p))
    lstep = jnp.broadcast_to(f32(log_step)[:, None, None], (g, HALO, LANES))
    b_c = lambda a: jnp.tile(f32(a), (1, 1, rep))
    b_r = lambda a: jnp.swapaxes(f32(a), 1, 2)
    c_c = lambda a: jnp.tile(jnp.swapaxes(f32(a), 1, 2), (1, 1, rep))

    def gspec(shape):
        return pl.BlockSpec((None,) + shape, lambda i: (i, 0, 0))

    return pl.pallas_call(
        functools.partial(_s5_kernel, chunks_per_seq=chunks_per_seq),
        out_shape=jax.ShapeDtypeStruct((g, rows, cw), BF16),
        grid=(g,),
        in_specs=[gspec((rows, cw)),
                  gspec((p, LANES)), gspec((p, LANES)), gspec((HALO, p)), gspec((HALO, p)),
                  gspec((HALO, LANES)),
                  gspec((p, LANES)), gspec((p, LANES)), gspec((hs, p)), gspec((hs, p)),
                  gspec((p, LANES)), gspec((p, LANES))],
        out_specs=gspec((rows, cw)),
        scratch_shapes=[pltpu.VMEM((cw, cw), BF16)],
        compiler_params=pltpu.CompilerParams(
            dimension_semantics=("parallel",), vmem_limit_bytes=VMEM_LIMIT),
        name="s5",
    )(u3, col(lam_re), col(lam_im), row(lam_re), row(lam_im), lstep,
      b_c(b_re), b_c(b_im), b_r(b_re), b_r(b_im), c_c(c_re), c_c(c_im))


def _merge_kernel(x_ref, gate_ref, ya_ref, ys_ref, su0_ref, su1_ref, su2_ref, sg0_ref, sg1_ref, sg2_ref, yc_ref,
                  sd_ref, gluw_ref, glub_ref, pa_ref, pb_ref, pc_ref, wout_ref, nw_ref, o_ref, *h_ref, final):
    d = x_ref.shape[-1]
    su = jnp.concatenate([su0_ref[...], su1_ref[...], su2_ref[...]], axis=1)
    ys = jax.nn.gelu(ys_ref[...].astype(F32) + sd_ref[...] * su)
    glu = jax.nn.sigmoid(_dot(ys.astype(BF16), gluw_ref[...]) + glub_ref[...])
    sgate = jnp.concatenate([sg0_ref[...], sg1_ref[...], sg2_ref[...]], axis=1)
    yb = ys * glu * _silu(sgate)
    pa = _dot(ya_ref[...].astype(BF16), pa_ref[...])
    pb = _dot(yb.astype(BF16), pb_ref[...])
    pc = _dot(yc_ref[...].astype(BF16), pc_ref[...])
    merged = (jax.nn.sigmoid(gate_ref[:, 0:d]) * pa
              + jax.nn.sigmoid(gate_ref[:, d:2 * d]) * pb
              + jax.nn.sigmoid(gate_ref[:, 2 * d:3 * d]) * pc)
    out = x_ref[...] + _dot(merged.astype(BF16), wout_ref[...])
    normed = out * lax.rsqrt(jnp.mean(out * out, axis=-1, keepdims=True) + NORM_EPS) * nw_ref[...]
    if final:
        o_ref[...] = normed.astype(o_ref.dtype)
    else:
        o_ref[...] = out.astype(o_ref.dtype)
        h_ref[0][...] = normed.astype(h_ref[0].dtype)


def _merge(x2d, proj2d, ya, ys, yc, s5_d, glu_w, glu_b, proj_a, proj_b, proj_c, w_out, next_norm_w, final, tm=256):
    t, d = x2d.shape
    tm = min(tm, t)
    sgw = S5_WIDTH // 3
    assert OFF_SG % sgw == 0 and OFF_SU % sgw == 0 and sgw % LANES == 0

    def wspec(shape):
        return pl.BlockSpec(shape, lambda i: (0, 0), pipeline_mode=pl.Buffered(1))

    row_spec = pl.BlockSpec((tm, d), lambda i: (i, 0))
    if final:
        out_shape, out_specs = jax.ShapeDtypeStruct((t, d), F32), row_spec
    else:
        out_shape = (jax.ShapeDtypeStruct((t, d), F32), jax.ShapeDtypeStruct((t, d), BF16))
        out_specs = (row_spec, row_spec)

    return pl.pallas_call(
        functools.partial(_merge_kernel, final=final),
        out_shape=out_shape,
        grid=(t // tm,),
        in_specs=[
            pl.BlockSpec((tm, d), lambda i: (i, 0)),
            pl.BlockSpec((tm, 3 * d), lambda i: (i, OFF_MERGE // (3 * d))),
            pl.BlockSpec((tm, GDN_WIDTH), lambda i: (i, 0)),
            pl.BlockSpec((tm, S5_WIDTH), lambda i: (i, 0)),
            pl.BlockSpec((tm, sgw), lambda i: (i, OFF_SU // sgw)),
            pl.BlockSpec((tm, sgw), lambda i: (i, OFF_SU // sgw + 1)),
            pl.BlockSpec((tm, sgw), lambda i: (i, OFF_SU // sgw + 2)),
            pl.BlockSpec((tm, sgw), lambda i: (i, OFF_SG // sgw)),
            pl.BlockSpec((tm, sgw), lambda i: (i, OFF_SG // sgw + 1)),
            pl.BlockSpec((tm, sgw), lambda i: (i, OFF_SG // sgw + 2)),
            pl.BlockSpec((tm, M2_WIDTH), lambda i: (i, 0)),
            wspec((1, S5_WIDTH)),
            wspec((S5_WIDTH, S5_WIDTH)), wspec((1, S5_WIDTH)),
            wspec((GDN_WIDTH, d)), wspec((S5_WIDTH, d)), wspec((M2_WIDTH, d)), wspec((d, d)),
            wspec((1, d)),
        ],
        out_specs=out_specs,
        compiler_params=pltpu.CompilerParams(
            dimension_semantics=("parallel",), vmem_limit_bytes=VMEM_LIMIT),
        name="merge",
    )(x2d, proj2d, ya, ys, proj2d, proj2d, proj2d, proj2d, proj2d, proj2d, yc,
      s5_d.astype(F32).reshape(1, S5_WIDTH), glu_w.astype(BF16), glu_b.astype(F32).reshape(1, -1),
      proj_a.astype(BF16), proj_b.astype(BF16), proj_c.astype(BF16), w_out.astype(BF16),
      next_norm_w.astype(F32).reshape(1, d))


def _w_in_segments(d_model):
    conv_dim = M2_WIDTH + 2 * M2_BC
    o = 0
    seg = {}
    for name, width in (("qkv", 3 * GDN_WIDTH), ("az", GDN_WIDTH), ("beta", GDN_HEADS), ("decay", GDN_HEADS),
                        ("su", S5_WIDTH), ("sg", S5_WIDTH), ("cz", M2_WIDTH), ("xbc", conv_dim),
                        ("dt", M2_HEADS), ("merge", 3 * d_model)):
        seg[name] = (o, width)
        o += width
    return seg


_MAIN_ORDER = (("merge", OFF_MERGE), ("qkv", OFF_QKV), ("xbc", OFF_XBC), ("az", OFF_AZ), ("cz", OFF_CZ),
               ("su", OFF_SU), ("sg", OFF_SG))


def _pack_kernel(w_ref, o_ref, small_ref, *, seg):
    for name, dst in _MAIN_ORDER:
        src, width = seg[name]
        o_ref[:, dst:dst + width] = w_ref[:, src:src + width].astype(o_ref.dtype)
    beta0, dt0 = seg["beta"][0], seg["dt"][0]
    assert beta0 % LANES == LANE_BETA and seg["decay"][0] == beta0 + GDN_HEADS and dt0 % LANES == LANE_DT
    lane = lax.broadcasted_iota(jnp.int32, (w_ref.shape[0], LANES), 1)
    t_gdn = w_ref[:, beta0 - LANE_BETA:beta0 - LANE_BETA + LANES]
    t_dt = w_ref[:, dt0 - LANE_DT:dt0 - LANE_DT + LANES]
    small = jnp.where(lane < LANE_DT, t_gdn, jnp.where(lane < LANE_DT + M2_HEADS, t_dt, 0.0))
    small_ref[...] = small.astype(small_ref.dtype)


def _pack_w_in(w_all, layer, tr=64):
    _, k, n = w_all.shape
    seg = _w_in_segments(k)
    return pl.pallas_call(
        functools.partial(_pack_kernel, seg=seg),
        out_shape=(jax.ShapeDtypeStruct((k, MAIN_W), BF16), jax.ShapeDtypeStruct((k, LANES), BF16)),
        grid=(k // tr,),
        in_specs=[pl.BlockSpec((None, tr, n), lambda i: (layer, i, 0))],
        out_specs=(pl.BlockSpec((tr, MAIN_W), lambda i: (i, 0)), pl.BlockSpec((tr, LANES), lambda i: (i, 0))),
        compiler_params=pltpu.CompilerParams(dimension_semantics=("parallel",), vmem_limit_bytes=VMEM_LIMIT),
        name="packw",
    )(w_all)


def _layer(x2d, h, bsz, seqlen, w_in_all, layer, gdn_conv_w, gdn_a_log, gdn_dt_bias, gdn_norm_w,
           s5_lam_re, s5_lam_im, s5_log_step, s5_b_re, s5_b_im, s5_c_re, s5_c_im, s5_d,
           s5_glu_w, s5_glu_b, m2_conv_w, m2_conv_b, m2_a_log, m2_dt_bias, m2_d, m2_norm_w,
           proj_a, proj_b, proj_c, w_out, next_norm_w, final):
    t = bsz * seqlen
    nc = seqlen // CHUNK
    w_main, w_small = _pack_w_in(w_in_all, layer)
    proj = _matmul(h, w_main, INPROJ_TM, INPROJ_TN)
    small = _matmul(h, w_small, INPROJ_TM, LANES)
    proj3 = proj.reshape(bsz, seqlen, MAIN_W)
    small3 = small.reshape(bsz, seqlen, LANES)

    ya = _gdn(proj3, small3, gdn_conv_w, gdn_a_log, gdn_dt_bias, gdn_norm_w)
    yc = _ssd(proj3, small3, m2_conv_w, m2_conv_b, m2_a_log, m2_dt_bias, m2_d, m2_norm_w)

    u = proj[:, OFF_SU:OFF_SU + S5_WIDTH].astype(BF16).reshape(bsz * nc, CHUNK, S5_GROUPS, S5_GROUP_SIZE)
    u = jnp.transpose(u, (2, 0, 1, 3)).reshape(S5_GROUPS, bsz * nc, CHUNK * S5_GROUP_SIZE)
    ys = _s5(u, s5_lam_re, s5_lam_im, s5_log_step, s5_b_re, s5_b_im, s5_c_re, s5_c_im, nc)
    ys = ys.reshape(S5_GROUPS, bsz * nc, CHUNK, S5_GROUP_SIZE)
    ys = jnp.transpose(ys, (1, 2, 0, 3)).reshape(t, S5_WIDTH)

    return _merge(x2d, proj, ya.reshape(t, GDN_WIDTH), ys, yc.reshape(t, M2_WIDTH), s5_d,
                  s5_glu_w, s5_glu_b, proj_a, proj_b, proj_c, w_out, next_norm_w, final)


def kernel(x, norm_w, w_in, gdn_conv_w, gdn_a_log, gdn_dt_bias, gdn_norm_w, s5_lam_re, s5_lam_im, s5_log_step, s5_b_re, s5_b_im, s5_c_re, s5_c_im, s5_d, s5_glu_w, s5_glu_b, m2_conv_w, m2_conv_b, m2_a_log, m2_dt_bias, m2_d, m2_norm_w, proj_a, proj_b, proj_c, w_out, final_norm_w):
    bsz, seqlen, d = x.shape
    depth = norm_w.shape[0]
    x2d = x.reshape(bsz * seqlen, d)
    h = _rmsnorm(x2d, norm_w[0], BF16)
    for i in range(depth):
        final = i == depth - 1
        res = _layer(x2d, h, bsz, seqlen, w_in, i, gdn_conv_w[i], gdn_a_log[i], gdn_dt_bias[i],
                     gdn_norm_w[i], s5_lam_re[i], s5_lam_im[i], s5_log_step[i], s5_b_re[i], s5_b_im[i],
                     s5_c_re[i], s5_c_im[i], s5_d[i], s5_glu_w[i], s5_glu_b[i],
                     m2_conv_w[i], m2_conv_b[i], m2_a_log[i], m2_dt_bias[i], m2_d[i], m2_norm_w[i],
                     proj_a[i], proj_b[i], proj_c[i], w_out[i],
                     final_norm_w if final else norm_w[i + 1], final)
        if final:
            return res.reshape(bsz, seqlen, d)
        x2d, h = res
```

```python
import functools
import math

import jax
import jax.numpy as jnp
from jax import lax
from jax.experimental import pallas as pl
from jax.experimental.pallas import tpu as pltpu

F32 = jnp.float32
BF16 = jnp.bfloat16
HIGHEST = lax.Precision.HIGHEST

NORM_EPS = 1e-6
CHUNK = 64
CONV_K = 4
HALO = 8
LANES = 128

GDN_HEADS = 8
GDN_HEAD_DIM = 128
GDN_WIDTH = GDN_HEADS * GDN_HEAD_DIM
S5_GROUP_SIZE = 16
S5_GROUPS = 48
S5_STATE = 64
S5_WIDTH = S5_GROUPS * S5_GROUP_SIZE
M2_HEADS = 16
M2_HEAD_DIM = 64
M2_WIDTH = M2_HEADS * M2_HEAD_DIM
M2_GROUPS = 4
M2_STATE = 128
M2_BC = M2_GROUPS * M2_STATE
M2_GROUP_W = M2_WIDTH // M2_GROUPS

OFF_MERGE = 0
OFF_QKV = OFF_MERGE + 3 * 2048
OFF_XBC = OFF_QKV + 3 * GDN_WIDTH
OFF_AZ = OFF_XBC + M2_WIDTH + 2 * M2_BC
OFF_CZ = OFF_AZ + GDN_WIDTH
OFF_SU = OFF_CZ + M2_WIDTH
OFF_SG = OFF_SU + S5_WIDTH
MAIN_W = OFF_SG + S5_WIDTH
LANE_BETA = 0
LANE_DECAY = GDN_HEADS
LANE_DT = 2 * GDN_HEADS

VMEM_LIMIT = 56 * 1024 * 1024
INPROJ_TM = 2048
INPROJ_TN = 512
GDN_CHUNKS_PER_STEP = 1


def _softplus(x):
    return jnp.maximum(x, 0.0) + jnp.log(1.0 + jnp.exp(-jnp.abs(x)))


def _silu(x):
    return x * jax.nn.sigmoid(x)


def _dot(a, b):
    return jnp.dot(a, b, preferred_element_type=F32)


def _dot_nt(a, b):
    return lax.dot_general(a, b, (((1,), (1,)), ((), ())), preferred_element_type=F32)


def _dot_hi(a, b):
    return jnp.dot(a, b, preferred_element_type=F32, precision=HIGHEST)


def _dot_b(a, b):
    return _dot(a.astype(BF16), b.astype(BF16))


def _tri(n, strict=False):
    r = lax.broadcasted_iota(jnp.int32, (n, n), 0)
    c = lax.broadcasted_iota(jnp.int32, (n, n), 1)
    return (r > c) if strict else (r >= c)


def _rmsnorm_kernel(x_ref, w_ref, o_ref):
    x = x_ref[...]
    y = x * lax.rsqrt(jnp.mean(x * x, axis=-1, keepdims=True) + NORM_EPS)
    o_ref[...] = (y * w_ref[...]).astype(o_ref.dtype)


def _rmsnorm(x2d, w, out_dtype, tm=512):
    t, d = x2d.shape
    tm = min(tm, t)
    return pl.pallas_call(
        _rmsnorm_kernel,
        out_shape=jax.ShapeDtypeStruct((t, d), out_dtype),
        grid=(t // tm,),
        in_specs=[pl.BlockSpec((tm, d), lambda i: (i, 0)),
                  pl.BlockSpec((1, d), lambda i: (0, 0))],
        out_specs=pl.BlockSpec((tm, d), lambda i: (i, 0)),
        compiler_params=pltpu.CompilerParams(dimension_semantics=("parallel",)),
        name="rmsnorm",
    )(x2d, w.reshape(1, d))


def _matmul_kernel(a_ref, b_ref, o_ref, *, rhs_transposed):
    dot = _dot_nt if rhs_transposed else _dot
    o_ref[...] = dot(a_ref[...], b_ref[...]).astype(o_ref.dtype)


def _matmul(a, b, tm, tn, out_dtype=F32, rhs_transposed=False):
    m, k = a.shape
    n = b.shape[0] if rhs_transposed else b.shape[1]
    tm = min(tm, m)
    tn = min(tn, n)
    b_spec = (pl.BlockSpec((tn, k), lambda i, j: (j, 0)) if rhs_transposed
              else pl.BlockSpec((k, tn), lambda i, j: (0, j)))
    return pl.pallas_call(
        functools.partial(_matmul_kernel, rhs_transposed=rhs_transposed),
        out_shape=jax.ShapeDtypeStruct((m, n), out_dtype),
        grid=(m // tm, n // tn),
        in_specs=[pl.BlockSpec((tm, k), lambda i, j: (i, 0)), b_spec],
        out_specs=pl.BlockSpec((tm, tn), lambda i, j: (i, j)),
        compiler_params=pltpu.CompilerParams(
            dimension_semantics=("parallel", "parallel"), vmem_limit_bytes=VMEM_LIMIT),
        name="inproj",
    )(a, b)


def _fill_ext(ext_ref, prev_ref, cur_ref, first):
    prev = prev_ref[...]
    ext_ref[0:HALO, :] = jnp.where(first, jnp.zeros_like(prev), prev)
    ext_ref[HALO:, :] = cur_ref[...]


def _conv_tile(ext_ref, w_ref, col, width, row0=0):
    acc = None
    for k in range(CONV_K):
        start = row0 + HALO - (CONV_K - 1) + k
        term = w_ref[k:k + 1, col:col + width] * ext_ref[start:start + CHUNK, col:col + width]
        acc = term if acc is None else acc + term
    return acc


def _gdn_kernel(q_ref, qprev_ref, z_ref, sm_ref, convw_ref, alog_ref, dtb_ref, normw_ref,
                o_ref, state_ref, ext_ref):
    c = pl.program_id(0)
    first = c == 0
    nb = q_ref.shape[0]
    nsub = q_ref.shape[1] // CHUNK
    dk = GDN_HEAD_DIM

    @pl.when(first)
    def _():
        state_ref[...] = jnp.zeros_like(state_ref)

    causal = _tri(CHUNK)
    strict = _tri(CHUNK, strict=True)
    ltri = causal.astype(F32)
    eye = (lax.broadcasted_iota(jnp.int32, (CHUNK, CHUNK), 0)
           == lax.broadcasted_iota(jnp.int32, (CHUNK, CHUNK), 1)).astype(F32)
    scale = dk ** -0.5

    for b in range(nb):
        _fill_ext(ext_ref.at[b], qprev_ref.at[b], q_ref.at[b], first)

    def per_seq(b, j):
        sm = sm_ref[b, j * CHUNK:(j + 1) * CHUNK, :]
        g_all = -jnp.exp(alog_ref[...]) * _softplus(sm + dtb_ref[...])
        gc_all = _dot_hi(ltri, g_all)
        glast_all = gc_all[CHUNK - 1:CHUNK, :]
        return dict(beta=jax.nn.sigmoid(sm), gc=gc_all, gc_t=gc_all.T, eg=jnp.exp(gc_all),
                    etail=jnp.exp(glast_all - gc_all), eglast=jnp.exp(glast_all))

    seqs = {(b, j): per_seq(b, j) for b in range(nb) for j in range(nsub)}

    items = [(j, b, h) for j in range(nsub) for b in range(nb) for h in range(GDN_HEADS)]
    idx = range(len(items))
    n_chain = nb * GDN_HEADS

    def prep(j, b, h):
        lo = h * dk
        sq = seqs[b, j]
        ext_b = ext_ref.at[b]
        q = _silu(_conv_tile(ext_b, convw_ref, lo, dk, j * CHUNK))
        k = _silu(_conv_tile(ext_b, convw_ref, GDN_WIDTH + lo, dk, j * CHUNK))
        v = _silu(_conv_tile(ext_b, convw_ref, 2 * GDN_WIDTH + lo, dk, j * CHUNK))
        q = q * lax.rsqrt(jnp.sum(q * q, axis=-1, keepdims=True) + NORM_EPS) * scale
        k = k * lax.rsqrt(jnp.sum(k * k, axis=-1, keepdims=True) + NORM_EPS)
        beta = sq["beta"][:, LANE_BETA + h:LANE_BETA + h + 1]
        ld = LANE_DECAY + h
        eg = sq["eg"][:, ld:ld + 1]
        decay = jnp.exp(jnp.where(causal, sq["gc"][:, ld:ld + 1] - sq["gc_t"][ld:ld + 1, :], -jnp.inf))
        kb = k * beta
        return dict(
            lhs_kk=jnp.concatenate([kb, q], axis=0).astype(BF16), k_b=k.astype(BF16), decay=decay,
            rhs_uw=jnp.concatenate([v * beta, kb * eg], axis=1).astype(BF16),
            q_dec=(q * eg).astype(BF16),
            k_tail_t=(k * sq["etail"][:, ld:ld + 1]).T.astype(BF16),
            eglast=sq["eglast"][:, ld:ld + 1])

    hd = [prep(j, b, h) for j, b, h in items]
    kk = [_dot_nt(d["lhs_kk"], d["k_b"]) for d in hd]
    qk = [(kk[i][CHUNK:] * hd[i]["decay"]).astype(BF16) for i in idx]

    b_pow = [-jnp.where(strict, kk[i][:CHUNK] * hd[i]["decay"], 0.0) for i in idx]
    x_inv = [eye + b for b in b_pow]
    b_pow = [_dot_b(b, b) for b in b_pow]
    n_sq = int(math.log2(CHUNK)) - 1
    for it in range(n_sq):
        if it < n_sq - 1:
            p = [_dot_b(jnp.concatenate([b_pow[i], x_inv[i]], axis=0), b_pow[i]) for i in idx]
            b_pow = [p[i][:CHUNK] for i in idx]
            x_inv = [x_inv[i] + p[i][CHUNK:] for i in idx]
        else:
            x_inv = [x_inv[i] + _dot_b(x_inv[i], b_pow[i]) for i in idx]

    uw = [_dot(x_inv[i].astype(BF16), hd[i]["rhs_uw"]) for i in idx]
    state = [state_ref[n] for n in range(n_chain)]
    out = [None] * len(items)
    for j in range(nsub):
        sub = range(j * n_chain, (j + 1) * n_chain)
        ws_qs = {i: _dot(jnp.concatenate([uw[i][:, dk:].astype(BF16), hd[i]["q_dec"]], axis=0),
                         state[i - j * n_chain].astype(BF16)) for i in sub}
        v_new = {i: (uw[i][:, :dk] - ws_qs[i][:CHUNK]).astype(BF16) for i in sub}
        for i in sub:
            out[i] = ws_qs[i][CHUNK:] + _dot(qk[i], v_new[i])
        state = [state[i - j * n_chain] * hd[i]["eglast"] + _dot(hd[i]["k_tail_t"], v_new[i]) for i in sub]
    for n in range(n_chain):
        state_ref[n] = state[n]
    for i, (j, b, h) in enumerate(items):
        lo = h * dk
        rows = slice(j * CHUNK, (j + 1) * CHUNK)
        o = out[i] * lax.rsqrt(jnp.mean(out[i] * out[i], axis=-1, keepdims=True) + NORM_EPS) * normw_ref[...]
        o_ref[b, rows, lo:lo + dk] = (o * _silu(z_ref[b, rows, lo:lo + dk])).astype(o_ref.dtype)


def _lane_vec(values, lane0):
    v = jnp.zeros((1, LANES), F32)
    return lax.dynamic_update_slice(v, values.astype(F32).reshape(1, -1), (0, lane0))


def _gdn(proj, small, conv_w, a_log, dt_bias, norm_w):
    bsz, seqlen, _ = proj.shape
    nc = seqlen // CHUNK
    wq = 3 * GDN_WIDTH
    nsub = GDN_CHUNKS_PER_STEP if nc % GDN_CHUNKS_PER_STEP == 0 else 1
    tb = nsub * CHUNK
    rows_per_halo = tb // HALO
    return pl.pallas_call(
        _gdn_kernel,
        out_shape=jax.ShapeDtypeStruct((bsz, seqlen, GDN_WIDTH), F32),
        grid=(nc // nsub,),
        in_specs=[
            pl.BlockSpec((bsz, tb, wq), lambda c: (0, c, OFF_QKV // wq)),
            pl.BlockSpec((bsz, HALO, wq),
                         lambda c: (0, jnp.maximum(c * rows_per_halo - 1, 0), OFF_QKV // wq)),
            pl.BlockSpec((bsz, tb, GDN_WIDTH), lambda c: (0, c, OFF_AZ // GDN_WIDTH)),
            pl.BlockSpec((bsz, tb, LANES), lambda c: (0, c, 0)),
            pl.BlockSpec((CONV_K, wq), lambda c: (0, 0)),
            pl.BlockSpec((1, LANES), lambda c: (0, 0)),
            pl.BlockSpec((1, LANES), lambda c: (0, 0)),
            pl.BlockSpec((1, GDN_HEAD_DIM), lambda c: (0, 0)),
        ],
        out_specs=pl.BlockSpec((bsz, tb, GDN_WIDTH), lambda c: (0, c, 0)),
        scratch_shapes=[pltpu.VMEM((bsz * GDN_HEADS, GDN_HEAD_DIM, GDN_HEAD_DIM), F32),
                        pltpu.VMEM((bsz, HALO + tb, wq), F32)],
        compiler_params=pltpu.CompilerParams(
            dimension_semantics=("arbitrary",), vmem_limit_bytes=VMEM_LIMIT),
        name="gdn",
    )(proj, proj, proj, small, conv_w.astype(F32), _lane_vec(a_log, LANE_DECAY),
      _lane_vec(dt_bias, LANE_DECAY), norm_w.astype(F32).reshape(1, GDN_HEAD_DIM))


def _ssd_kernel(x_ref, bc_ref, xprev_ref, bcprev_ref, z_ref, sm_ref, convw_ref, convb_ref,
                alog_ref, dtb_ref, dskip_ref, normw_ref, expand_ref,
                o_ref, state_ref, ext_ref):
    c = pl.program_id(0)
    first = c == 0
    nb = x_ref.shape[0]

    @pl.when(first)
    def _():
        state_ref[...] = jnp.zeros_like(state_ref)

    causal = _tri(CHUNK)
    ltri = causal.astype(F32)
    expand = expand_ref[...].astype(F32)

    def per_seq(b):
        prev_x = xprev_ref[b]
        prev_bc = bcprev_ref[b]
        ext_ref[b, 0:HALO, 0:M2_WIDTH] = jnp.where(first, jnp.zeros_like(prev_x), prev_x)
        ext_ref[b, 0:HALO, M2_WIDTH:] = jnp.where(first, jnp.zeros_like(prev_bc), prev_bc)
        ext_ref[b, HALO:, 0:M2_WIDTH] = x_ref[b]
        ext_ref[b, HALO:, M2_WIDTH:] = bc_ref[b]
        dt_all = _softplus(sm_ref[b] + dtb_ref[...])
        a_all = -jnp.exp(alog_ref[...]) * dt_all
        acum = _dot_hi(ltri, a_all)
        ea = jnp.exp(acum)
        ds = jnp.exp(acum[CHUNK - 1:CHUNK, :] - acum)
        spread = _dot_hi(jnp.concatenate([dt_all, ea, ds], axis=0), expand)
        ea_x = spread[CHUNK:2 * CHUNK]
        return dict(acum=acum, acum_t=acum.T, dt_x=spread[:CHUNK], ea_x=ea_x, ds_x=spread[2 * CHUNK:],
                    cdecay_x=ea_x[CHUNK - 1:CHUNK, :])

    seqs = [per_seq(b) for b in range(nb)]

    lane = lax.broadcasted_iota(jnp.int32, (CHUNK, 2 * M2_HEAD_DIM), 1)
    heads_per_group = M2_HEADS // M2_GROUPS

    items = [(b, g) for b in range(nb) for g in range(M2_GROUPS)]
    idx = range(len(items))

    def prep(b, g):
        glo = g * M2_GROUP_W
        bcol = M2_WIDTH + g * M2_STATE
        ccol = M2_WIDTH + M2_BC + g * M2_STATE
        ext_b = ext_ref.at[b]
        bm = _silu(_conv_tile(ext_b, convw_ref, bcol, M2_STATE) + convb_ref[:, bcol:bcol + M2_STATE])
        cm = _silu(_conv_tile(ext_b, convw_ref, ccol, M2_STATE) + convb_ref[:, ccol:ccol + M2_STATE])
        xs = _silu(_conv_tile(ext_b, convw_ref, glo, M2_GROUP_W) + convb_ref[:, glo:glo + M2_GROUP_W])
        xdt = xs * seqs[b]["dt_x"][:, glo:glo + M2_GROUP_W]
        return dict(xs=xs, xdt_b=xdt.astype(BF16),
                    xds=(xdt * seqs[b]["ds_x"][:, glo:glo + M2_GROUP_W]).astype(BF16),
                    bm_b=bm.astype(BF16), bm_t=bm.T.astype(BF16), cm_b=cm.astype(BF16))

    gd = [prep(b, g) for b, g in items]
    scores = [_dot_nt(d["cm_b"], d["bm_b"]) for d in gd]
    s_prev = [state_ref[i] for i in idx]
    y_off = [_dot(gd[i]["cm_b"], s_prev[i].astype(BF16)) for i in idx]
    states = [_dot(d["bm_t"], d["xds"]) for d in gd]
    for i, (b, g) in enumerate(items):
        glo = g * M2_GROUP_W
        state_ref[i] = s_prev[i] * seqs[b]["cdecay_x"][:, glo:glo + M2_GROUP_W] + states[i]

    def diag(i, j, r):
        b, g = items[i]
        la = LANE_DT + g * heads_per_group + 2 * j + r
        seg = jnp.exp(jnp.where(causal, seqs[b]["acum"][:, la:la + 1] - seqs[b]["acum_t"][la:la + 1, :],
                                -jnp.inf))
        x2 = gd[i]["xdt_b"][:, j * 2 * M2_HEAD_DIM:(j + 1) * 2 * M2_HEAD_DIM]
        return _dot((scores[i] * seg).astype(BF16), x2)

    pairs = range(heads_per_group // 2)
    yd = [[[diag(i, j, r) for r in range(2)] for j in pairs] for i in idx]
    for i, (b, g) in enumerate(items):
        glo = g * M2_GROUP_W
        parts = [jnp.where(lane < M2_HEAD_DIM, yd[i][j][0], yd[i][j][1]) for j in pairs]
        y = jnp.concatenate(parts, axis=1) + y_off[i] * seqs[b]["ea_x"][:, glo:glo + M2_GROUP_W]
        y = y + dskip_ref[:, glo:glo + M2_GROUP_W] * gd[i]["xs"]
        y = y * _silu(z_ref[b, :, glo:glo + M2_GROUP_W])
        y = y * lax.rsqrt(jnp.mean(y * y, axis=-1, keepdims=True) + NORM_EPS)
        o_ref[b, :, glo:glo + M2_GROUP_W] = (y * normw_ref[:, glo:glo + M2_GROUP_W]).astype(o_ref.dtype)


def _ssd(proj, small, conv_w, conv_b, a_log, dt_bias, d_skip, norm_w):
    bsz, seqlen, _ = proj.shape
    nc = seqlen // CHUNK
    rows_per_halo = CHUNK // HALO
    conv_dim = M2_WIDTH + 2 * M2_BC
    head_of_col = jnp.arange(M2_WIDTH) // M2_HEAD_DIM
    expand = (jnp.arange(LANES)[:, None] == (LANE_DT + head_of_col)[None, :]).astype(BF16)
    dskip_x = jnp.repeat(d_skip.astype(F32), M2_HEAD_DIM).reshape(1, M2_WIDTH)
    xi = OFF_XBC // M2_WIDTH

    def halo_idx(c):
        return jnp.maximum(c * rows_per_halo - 1, 0)

    return pl.pallas_call(
        _ssd_kernel,
        out_shape=jax.ShapeDtypeStruct((bsz, seqlen, M2_WIDTH), F32),
        grid=(nc,),
        in_specs=[
            pl.BlockSpec((bsz, CHUNK, M2_WIDTH), lambda c: (0, c, xi)),
            pl.BlockSpec((bsz, CHUNK, 2 * M2_BC), lambda c: (0, c, xi + 1)),
            pl.BlockSpec((bsz, HALO, M2_WIDTH), lambda c: (0, halo_idx(c), xi)),
            pl.BlockSpec((bsz, HALO, 2 * M2_BC), lambda c: (0, halo_idx(c), xi + 1)),
            pl.BlockSpec((bsz, CHUNK, M2_WIDTH), lambda c: (0, c, OFF_CZ // M2_WIDTH)),
            pl.BlockSpec((bsz, CHUNK, LANES), lambda c: (0, c, 0)),
            pl.BlockSpec((CONV_K, conv_dim), lambda c: (0, 0)),
            pl.BlockSpec((1, conv_dim), lambda c: (0, 0)),
            pl.BlockSpec((1, LANES), lambda c: (0, 0)),
            pl.BlockSpec((1, LANES), lambda c: (0, 0)),
            pl.BlockSpec((1, M2_WIDTH), lambda c: (0, 0)),
            pl.BlockSpec((1, M2_WIDTH), lambda c: (0, 0)),
            pl.BlockSpec((LANES, M2_WIDTH), lambda c: (0, 0)),
        ],
        out_specs=pl.BlockSpec((bsz, CHUNK, M2_WIDTH), lambda c: (0, c, 0)),
        scratch_shapes=[pltpu.VMEM((bsz * M2_GROUPS, M2_STATE, M2_GROUP_W), F32),
                        pltpu.VMEM((bsz, HALO + CHUNK, conv_dim), F32)],
        compiler_params=pltpu.CompilerParams(
            dimension_semantics=("arbitrary",), vmem_limit_bytes=VMEM_LIMIT),
        name="ssd",
    )(proj, proj, proj, proj, proj, small, conv_w.astype(F32), conv_b.astype(F32).reshape(1, conv_dim),
      _lane_vec(a_log, LANE_DT), _lane_vec(dt_bias, LANE_DT), dskip_x,
      norm_w.astype(F32).reshape(1, M2_WIDTH), expand)


def _s5_kernel(u_ref, lamre_c_ref, lamim_c_ref, lamre_r_ref, lamim_r_ref, lstep_ref,
               bre_c_ref, bim_c_ref, bre_r_ref, bim_r_ref, cre_c_ref, cim_c_ref,
               o_ref, toep_ref, *, chunks_per_seq):
    p = S5_STATE
    hs = S5_GROUP_SIZE
    cw = CHUNK * hs
    step = jnp.exp(lstep_ref[0:1, :])
    step_r = step[:, :p]

    def disc(lre, lim, st):
        lre = jnp.minimum(lre, -1e-4)
        mag = jnp.exp(lre * st)
        are = mag * jnp.cos(lim * st)
        aim = mag * jnp.sin(lim * st)
        den = lre * lre + lim * lim
        fre = ((are - 1.0) * lre + aim * lim) / den
        fim = (aim * lre - (are - 1.0) * lim) / den
        return lre, are, aim, fre, fim

    lre_c, are_c, aim_c, fre_c, fim_c = disc(lamre_c_ref[...], lamim_c_ref[...], step)
    lim_c = lamim_c_ref[...]
    bbre_c = fre_c * bre_c_ref[...] - fim_c * bim_c_ref[...]
    bbim_c = fre_c * bim_c_ref[...] + fim_c * bre_c_ref[...]
    lre_r, _, _, fre_r, fim_r = disc(lamre_r_ref[0:1, :], lamim_r_ref[0:1, :], step_r)
    lim_r = lamim_r_ref[0:1, :]
    bbre_r = fre_r * bre_r_ref[...] - fim_r * bim_r_ref[...]
    bbim_r = fre_r * bim_r_ref[...] + fim_r * bre_r_ref[...]

    per = LANES // hs
    ntile = cw // LANES
    r_lane = lax.shift_right_logical(lax.broadcasted_iota(jnp.int32, (p, LANES), 1),
                                     int(math.log2(hs))).astype(F32)

    def cmul(a, b):
        return a[0] * b[0] - a[1] * b[1], a[0] * b[1] + a[1] * b[0]

    def cpow(expo):
        mag = jnp.exp(lre_c * step * expo)
        ang = lim_c * step * expo
        return mag * jnp.cos(ang), mag * jnp.sin(ang)

    a_c = (are_c, aim_c)
    c_fwd = cmul((cre_c_ref[...], cim_c_ref[...]), cpow(r_lane))
    c_fwd1 = cmul(c_fwd, a_c)
    b_rev = cmul((bbre_c, bbim_c), cpow(float(per - 1) - r_lane))
    a_per = cpow(float(per))
    tile_pow = [(jnp.ones_like(are_c), jnp.zeros_like(are_c))]
    for _ in range(ntile - 1):
        tile_pow.append(cmul(tile_pow[-1], a_per))
    m0 = [cmul(c_fwd, tile_pow[k]) for k in range(ntile)]
    m1 = [cmul(c_fwd1, tile_pow[k]) for k in range(ntile)]
    nn = [cmul(b_rev, tile_pow[ntile - 1 - k]) for k in range(ntile)]
    m0re = jnp.concatenate([t[0] for t in m0], axis=1)
    m0im = jnp.concatenate([t[1] for t in m0], axis=1)
    m1re = jnp.concatenate([t[0] for t in m1], axis=1)
    m1im = jnp.concatenate([t[1] for t in m1], axis=1)
    nre = jnp.concatenate([t[0] for t in nn], axis=1)
    nim = jnp.concatenate([t[1] for t in nn], axis=1)

    kt = _dot_hi(bbre_r, m0re) - _dot_hi(bbim_r, m0im)
    lane_k = lax.broadcasted_iota(jnp.int32, (hs, cw), 1)

    for r in range(per):
        kr = pltpu.roll(kt, r * hs, 1) if r else kt
        for q in range(ntile):
            s = q * per + r
            cut = cw - q * LANES
            rolled = kr if q == 0 else jnp.concatenate([kr[:, cut:], kr[:, :cut]], axis=1)
            toep_ref[s * hs:(s + 1) * hs, :] = jnp.where(lane_k >= s * hs, rolled, 0.0).astype(BF16)

    ub = u_ref[...]
    y = _dot(ub, toep_ref[...])
    sre = _dot_nt(ub, nre.astype(BF16))
    sim = _dot_nt(ub, nim.astype(BF16))

    rows = ub.shape[0]
    cidx = lax.broadcasted_iota(jnp.int32, (rows, p), 0) & (chunks_per_seq - 1)
    mag_c = jnp.exp(lre_r * step_r * float(CHUNK))
    pre = mag_c * jnp.cos(lim_r * step_r * float(CHUNK))
    pim = mag_c * jnp.sin(lim_r * step_r * float(CHUNK))
    d = 1
    while d < chunks_per_seq:
        ok = cidx >= d
        shre = jnp.where(ok, pltpu.roll(sre, d, 0), 0.0)
        shim = jnp.where(ok, pltpu.roll(sim, d, 0), 0.0)
        sre, sim = sre + pre * shre - pim * shim, sim + pre * shim + pim * shre
        pre, pim = pre * pre - pim * pim, 2.0 * pre * pim
        d *= 2
    ok = cidx >= 1
    xsre = jnp.where(ok, pltpu.roll(sre, 1, 0), 0.0)
    xsim = jnp.where(ok, pltpu.roll(sim, 1, 0), 0.0)
    y = y + _dot(xsre.astype(BF16), m1re.astype(BF16)) - _dot(xsim.astype(BF16), m1im.astype(BF16))
    o_ref[...] = y.astype(o_ref.dtype)


def _s5(u3, lam_re, lam_im, log_step, b_re, b_im, c_re, c_im, chunks_per_seq):
    g, rows, cw = u3.shape
    p, hs = S5_STATE, S5_GROUP_SIZE
    rep = LANES // hs
    f32 = lambda a: a.astype(F32)
    col = lambda a: jnp.broadcast_to(f32(a)[:, :, None], (g, p, LANES))
    row = lambda a: jnp.broadcast_to(f32(a)[:, None, :], (g, HALO, p))
    lstep = jnp.broadcast_to(f32(log_step)[:, None, None], (g, HALO, LANES))
    b_c = lambda a: jnp.tile(f32(a), (1, 1, rep))
    b_r = lambda a: jnp.swapaxes(f32(a), 1, 2)
    c_c = lambda a: jnp.tile(jnp.swapaxes(f32(a), 1, 2), (1, 1, rep))

    def gspec(shape):
        return pl.BlockSpec((None,) + shape, lambda i: (i, 0, 0))

    return pl.pallas_call(
        functools.partial(_s5_kernel, chunks_per_seq=chunks_per_seq),
        out_shape=jax.ShapeDtypeStruct((g, rows, cw), BF16),
        grid=(g,),
        in_specs=[gspec((rows, cw)),
                  gspec((p, LANES)), gspec((p, LANES)), gspec((HALO, p)), gspec((HALO, p)),
                  gspec((HALO, LANES)),
                  gspec((p, LANES)), gspec((p, LANES)), gspec((hs, p)), gspec((hs, p)),
                  gspec((p, LANES)), gspec((p, LANES))],
        out_specs=gspec((rows, cw)),
        scratch_shapes=[pltpu.VMEM((cw, cw), BF16)],
        compiler_params=pltpu.CompilerParams(
            dimension_semantics=("parallel",), vmem_limit_bytes=VMEM_LIMIT),
        name="s5",
    )(u3, col(lam_re), col(lam_im), row(lam_re), row(lam_im), lstep,
      b_c(b_re), b_c(b_im), b_r(b_re), b_r(b_im), c_c(c_re), c_c(c_im))


def _merge_kernel(x_ref, gate_ref, ya_ref, ys_ref, su0_ref, su1_ref, su2_ref, sg0_ref, sg1_ref, sg2_ref, yc_ref,
                  sd_ref, gluw_ref, glub_ref, pa_ref, pb_ref, pc_ref, wout_ref, nw_ref, o_ref, *h_ref, final):
    d = x_ref.shape[-1]
    su = jnp.concatenate([su0_ref[...], su1_ref[...], su2_ref[...]], axis=1)
    ys = jax.nn.gelu(ys_ref[...].astype(F32) + sd_ref[...] * su)
    glu = jax.nn.sigmoid(_dot(ys.astype(BF16), gluw_ref[...]) + glub_ref[...])
    sgate = jnp.concatenate([sg0_ref[...], sg1_ref[...], sg2_ref[...]], axis=1)
    yb = ys * glu * _silu(sgate)
    pa = _dot(ya_ref[...].astype(BF16), pa_ref[...])
    pb = _dot(yb.astype(BF16), pb_ref[...])
    pc = _dot(yc_ref[...].astype(BF16), pc_ref[...])
    merged = (jax.nn.sigmoid(gate_ref[:, 0:d]) * pa
              + jax.nn.sigmoid(gate_ref[:, d:2 * d]) * pb
              + jax.nn.sigmoid(gate_ref[:, 2 * d:3 * d]) * pc)
    out = x_ref[...] + _dot(merged.astype(BF16), wout_ref[...])
    normed = out * lax.rsqrt(jnp.mean(out * out, axis=-1, keepdims=True) + NORM_EPS) * nw_ref[...]
    if final:
        o_ref[...] = normed.astype(o_ref.dtype)
    else:
        o_ref[...] = out.astype(o_ref.dtype)
        h_ref[0][...] = normed.astype(h_ref[0].dtype)


def _merge(x2d, proj2d, ya, ys, yc, s5_d, glu_w, glu_b, proj_a, proj_b, proj_c, w_out, next_norm_w, final, tm=256):
    t, d = x2d.shape
    tm = min(tm, t)
    sgw = S5_WIDTH // 3
    assert OFF_SG % sgw == 0 and OFF_SU % sgw == 0 and sgw % LANES == 0

    def wspec(shape):
        return pl.BlockSpec(shape, lambda i: (0, 0), pipeline_mode=pl.Buffered(1))

    row_spec = pl.BlockSpec((tm, d), lambda i: (i, 0))
    if final:
        out_shape, out_specs = jax.ShapeDtypeStruct((t, d), F32), row_spec
    else:
        out_shape = (jax.ShapeDtypeStruct((t, d), F32), jax.ShapeDtypeStruct((t, d), BF16))
        out_specs = (row_spec, row_spec)

    return pl.pallas_call(
        functools.partial(_merge_kernel, final=final),
        out_shape=out_shape,
        grid=(t // tm,),
        in_specs=[
            pl.BlockSpec((tm, d), lambda i: (i, 0)),
            pl.BlockSpec((tm, 3 * d), lambda i: (i, OFF_MERGE // (3 * d))),
            pl.BlockSpec((tm, GDN_WIDTH), lambda i: (i, 0)),
            pl.BlockSpec((tm, S5_WIDTH), lambda i: (i, 0)),
            pl.BlockSpec((tm, sgw), lambda i: (i, OFF_SU // sgw)),
            pl.BlockSpec((tm, sgw), lambda i: (i, OFF_SU // sgw + 1)),
            pl.BlockSpec((tm, sgw), lambda i: (i, OFF_SU // sgw + 2)),
            pl.BlockSpec((tm, sgw), lambda i: (i, OFF_SG // sgw)),
            pl.BlockSpec((tm, sgw), lambda i: (i, OFF_SG // sgw + 1)),
            pl.BlockSpec((tm, sgw), lambda i: (i, OFF_SG // sgw + 2)),
            pl.BlockSpec((tm, M2_WIDTH), lambda i: (i, 0)),
            wspec((1, S5_WIDTH)),
            wspec((S5_WIDTH, S5_WIDTH)), wspec((1, S5_WIDTH)),
            wspec((GDN_WIDTH, d)), wspec((S5_WIDTH, d)), wspec((M2_WIDTH, d)), wspec((d, d)),
            wspec((1, d)),
        ],
        out_specs=out_specs,
        compiler_params=pltpu.CompilerParams(
            dimension_semantics=("parallel",), vmem_limit_bytes=VMEM_LIMIT),
        name="merge",
    )(x2d, proj2d, ya, ys, proj2d, proj2d, proj2d, proj2d, proj2d, proj2d, yc,
      s5_d.astype(F32).reshape(1, S5_WIDTH), glu_w.astype(BF16), glu_b.astype(F32).reshape(1, -1),
      proj_a.astype(BF16), proj_b.astype(BF16), proj_c.astype(BF16), w_out.astype(BF16),
      next_norm_w.astype(F32).reshape(1, d))


def _w_in_segments(d_model):
    conv_dim = M2_WIDTH + 2 * M2_BC
    o = 0
    seg = {}
    for name, width in (("qkv", 3 * GDN_WIDTH), ("az", GDN_WIDTH), ("beta", GDN_HEADS), ("decay", GDN_HEADS),
                        ("su", S5_WIDTH), ("sg", S5_WIDTH), ("cz", M2_WIDTH), ("xbc", conv_dim),
                        ("dt", M2_HEADS), ("merge", 3 * d_model)):
        seg[name] = (o, width)
        o += width
    return seg


_MAIN_ORDER = (("merge", OFF_MERGE), ("qkv", OFF_QKV), ("xbc", OFF_XBC), ("az", OFF_AZ), ("cz", OFF_CZ),
               ("su", OFF_SU), ("sg", OFF_SG))


PACK_COLS = 256


def _pack_main_kernel(src_ref, wt_ref, o_ref):
    del src_ref
    o_ref[...] = wt_ref[...].T.astype(o_ref.dtype)


def _pack_small_kernel(gdn_ref, dt_ref, o_ref):
    o_ref[...] = jnp.zeros_like(o_ref)
    o_ref[LANE_BETA:LANE_BETA + 2 * GDN_HEADS, :] = gdn_ref[...].astype(o_ref.dtype)
    o_ref[LANE_DT:LANE_DT + M2_HEADS, :] = dt_ref[...].astype(o_ref.dtype)


def _pack_w_in(wt_all, layer):
    _, n, k = wt_all.shape
    seg = _w_in_segments(k)
    src_rows = []
    for j in range(MAIN_W // PACK_COLS):
        col = j * PACK_COLS
        name, dst = [(nm, d0) for nm, d0 in _MAIN_ORDER if d0 <= col < d0 + seg[nm][1]][0]
        src_rows.append(seg[name][0] + col - dst)
    main = pl.pallas_call(
        _pack_main_kernel,
        out_shape=jax.ShapeDtypeStruct((k, MAIN_W), BF16),
        grid_spec=pltpu.PrefetchScalarGridSpec(
            num_scalar_prefetch=1,
            grid=(MAIN_W // PACK_COLS,),
            in_specs=[pl.BlockSpec((None, pl.Element(PACK_COLS), pl.Element(k)),
                                   lambda j, src: (layer, pl.multiple_of(src[j], HALO), 0))],
            out_specs=pl.BlockSpec((k, PACK_COLS), lambda j, src: (0, j))),
        compiler_params=pltpu.CompilerParams(dimension_semantics=("arbitrary",), vmem_limit_bytes=VMEM_LIMIT),
        name="packw",
    )(jnp.asarray(src_rows, jnp.int32), wt_all)
    n_gdn = 2 * GDN_HEADS
    assert seg["decay"][0] == seg["beta"][0] + GDN_HEADS
    small_t = pl.pallas_call(
        _pack_small_kernel,
        out_shape=jax.ShapeDtypeStruct((LANES, k), BF16),
        grid=(1,),
        in_specs=[pl.BlockSpec((None, pl.Element(n_gdn), pl.Element(k)), lambda i: (layer, seg["beta"][0], 0)),
                  pl.BlockSpec((None, pl.Element(M2_HEADS), pl.Element(k)), lambda i: (layer, seg["dt"][0], 0))],
        out_specs=pl.BlockSpec((LANES, k), lambda i: (0, 0)),
        name="packs",
    )(wt_all, wt_all)
    return main, small_t


def _layer(x2d, h, bsz, seqlen, w_in_all, layer, gdn_conv_w, gdn_a_log, gdn_dt_bias, gdn_norm_w,
           s5_lam_re, s5_lam_im, s5_log_step, s5_b_re, s5_b_im, s5_c_re, s5_c_im, s5_d,
           s5_glu_w, s5_glu_b, m2_conv_w, m2_conv_b, m2_a_log, m2_dt_bias, m2_d, m2_norm_w,
           proj_a, proj_b, proj_c, w_out, next_norm_w, final):
    t = bsz * seqlen
    nc = seqlen // CHUNK
    w_main, w_small_t = _pack_w_in(w_in_all, layer)
    proj = _matmul(h, w_main, INPROJ_TM, INPROJ_TN)
    small = _matmul(h, w_small_t, INPROJ_TM, LANES, rhs_transposed=True)
    proj3 = proj.reshape(bsz, seqlen, MAIN_W)
    small3 = small.reshape(bsz, seqlen, LANES)

    ya = _gdn(proj3, small3, gdn_conv_w, gdn_a_log, gdn_dt_bias, gdn_norm_w)
    yc = _ssd(proj3, small3, m2_conv_w, m2_conv_b, m2_a_log, m2_dt_bias, m2_d, m2_norm_w)

    u = proj[:, OFF_SU:OFF_SU + S5_WIDTH].astype(BF16).reshape(bsz * nc, CHUNK, S5_GROUPS, S5_GROUP_SIZE)
    u = jnp.transpose(u, (2, 0, 1, 3)).reshape(S5_GROUPS, bsz * nc, CHUNK * S5_GROUP_SIZE)
    ys = _s5(u, s5_lam_re, s5_lam_im, s5_log_step, s5_b_re, s5_b_im, s5_c_re, s5_c_im, nc)
    ys = ys.reshape(S5_GROUPS, bsz * nc, CHUNK, S5_GROUP_SIZE)
    ys = jnp.transpose(ys, (1, 2, 0, 3)).reshape(t, S5_WIDTH)

    return _merge(x2d, proj, ya.reshape(t, GDN_WIDTH), ys, yc.reshape(t, M2_WIDTH), s5_d,
                  s5_glu_w, s5_glu_b, proj_a, proj_b, proj_c, w_out, next_norm_w, final)


def kernel(x, norm_w, w_in, gdn_conv_w, gdn_a_log, gdn_dt_bias, gdn_norm_w, s5_lam_re, s5_lam_im, s5_log_step, s5_b_re, s5_b_im, s5_c_re, s5_c_im, s5_d, s5_glu_w, s5_glu_b, m2_conv_w, m2_conv_b, m2_a_log, m2_dt_bias, m2_d, m2_norm_w, proj_a, proj_b, proj_c, w_out, final_norm_w):
    bsz, seqlen, d = x.shape
    depth = norm_w.shape[0]
    x2d = x.reshape(bsz * seqlen, d)
    h = _rmsnorm(x2d, norm_w[0], BF16)
    w_in = jnp.swapaxes(w_in, 1, 2)
    for i in range(depth):
        final = i == depth - 1
        res = _layer(x2d, h, bsz, seqlen, w_in, i, gdn_conv_w[i], gdn_a_log[i], gdn_dt_bias[i],
                     gdn_norm_w[i], s5_lam_re[i], s5_lam_im[i], s5_log_step[i], s5_b_re[i], s5_b_im[i],
                     s5_c_re[i], s5_c_im[i], s5_d[i], s5_glu_w[i], s5_glu_b[i],
                     m2_conv_w[i], m2_conv_b[i], m2_a_log[i], m2_dt_bias[i], m2_d[i], m2_norm_w[i],
                     proj_a[i], proj_b[i], proj_c[i], w_out[i],
                     final_norm_w if final else norm_w[i + 1], final)
        if final:
            return res.reshape(bsz, seqlen, d)
        x2d, h = res
```

```python
import functools
import math

import jax
import jax.numpy as jnp
from jax import lax
from jax.experimental import pallas as pl
from jax.experimental.pallas import tpu as pltpu

F32 = jnp.float32
BF16 = jnp.bfloat16
HIGHEST = lax.Precision.HIGHEST

NORM_EPS = 1e-6
CHUNK = 64
CONV_K = 4
HALO = 8
LANES = 128

GDN_HEADS = 8
GDN_HEAD_DIM = 128
GDN_WIDTH = GDN_HEADS * GDN_HEAD_DIM
S5_GROUP_SIZE = 16
S5_GROUPS = 48
S5_STATE = 64
S5_WIDTH = S5_GROUPS * S5_GROUP_SIZE
M2_HEADS = 16
M2_HEAD_DIM = 64
M2_WIDTH = M2_HEADS * M2_HEAD_DIM
M2_GROUPS = 4
M2_STATE = 128
M2_BC = M2_GROUPS * M2_STATE
M2_GROUP_W = M2_WIDTH // M2_GROUPS

OFF_MERGE = 0
OFF_QKV = OFF_MERGE + 3 * 2048
OFF_XBC = OFF_QKV + 3 * GDN_WIDTH
OFF_AZ = OFF_XBC + M2_WIDTH + 2 * M2_BC
OFF_CZ = OFF_AZ + GDN_WIDTH
OFF_SU = OFF_CZ + M2_WIDTH
OFF_SG = OFF_SU + S5_WIDTH
MAIN_W = OFF_SG + S5_WIDTH
LANE_BETA = 0
LANE_DECAY = GDN_HEADS
LANE_DT = 2 * GDN_HEADS

VMEM_LIMIT = 56 * 1024 * 1024
INPROJ_TM = 2048
INPROJ_TN = 512
GDN_CHUNKS_PER_STEP = 1


def _softplus(x):
    return jnp.maximum(x, 0.0) + jnp.log(1.0 + jnp.exp(-jnp.abs(x)))


def _sigmoid(x):
    return 0.5 * jnp.tanh(0.5 * x) + 0.5


def _silu(x):
    h = 0.5 * x
    return h + h * jnp.tanh(h)


def _dot(a, b):
    return jnp.dot(a, b, preferred_element_type=F32)


def _dot_nt(a, b):
    return lax.dot_general(a, b, (((1,), (1,)), ((), ())), preferred_element_type=F32)


def _dot_hi(a, b):
    return jnp.dot(a, b, preferred_element_type=F32, precision=HIGHEST)


def _dot_b(a, b):
    return _dot(a.astype(BF16), b.astype(BF16))


def _split_bf16(x, terms):
    parts = []
    for _ in range(terms):
        p = x.astype(BF16)
        parts.append(p)
        x = x - p.astype(F32)
    return parts


def _cumsum_rows(ltri_b, x):
    n = x.shape[1]
    s = _dot(ltri_b, jnp.concatenate(_split_bf16(x, 3), axis=1))
    return s[:, :n] + s[:, n:2 * n] + s[:, 2 * n:]


def _tri(n, strict=False):
    r = lax.broadcasted_iota(jnp.int32, (n, n), 0)
    c = lax.broadcasted_iota(jnp.int32, (n, n), 1)
    return (r > c) if strict else (r >= c)


def _rmsnorm_kernel(x_ref, w_ref, o_ref):
    x = x_ref[...]
    y = x * lax.rsqrt(jnp.mean(x * x, axis=-1, keepdims=True) + NORM_EPS)
    o_ref[...] = (y * w_ref[...]).astype(o_ref.dtype)


def _rmsnorm(x2d, w, out_dtype, tm=512):
    t, d = x2d.shape
    tm = min(tm, t)
    return pl.pallas_call(
        _rmsnorm_kernel,
        out_shape=jax.ShapeDtypeStruct((t, d), out_dtype),
        grid=(t // tm,),
        in_specs=[pl.BlockSpec((tm, d), lambda i: (i, 0)),
                  pl.BlockSpec((1, d), lambda i: (0, 0))],
        out_specs=pl.BlockSpec((tm, d), lambda i: (i, 0)),
        compiler_params=pltpu.CompilerParams(dimension_semantics=("parallel",)),
        name="rmsnorm",
    )(x2d, w.reshape(1, d))


def _matmul_kernel(a_ref, b_ref, o_ref, *, rhs_transposed):
    dot = _dot_nt if rhs_transposed else _dot
    o_ref[...] = dot(a_ref[...], b_ref[...]).astype(o_ref.dtype)


def _matmul(a, b, tm, tn, out_dtype=F32, rhs_transposed=False):
    m, k = a.shape
    n = b.shape[0] if rhs_transposed else b.shape[1]
    tm = min(tm, m)
    tn = min(tn, n)
    b_spec = (pl.BlockSpec((tn, k), lambda i, j: (j, 0)) if rhs_transposed
              else pl.BlockSpec((k, tn), lambda i, j: (0, j)))
    return pl.pallas_call(
        functools.partial(_matmul_kernel, rhs_transposed=rhs_transposed),
        out_shape=jax.ShapeDtypeStruct((m, n), out_dtype),
        grid=(m // tm, n // tn),
        in_specs=[pl.BlockSpec((tm, k), lambda i, j: (i, 0)), b_spec],
        out_specs=pl.BlockSpec((tm, tn), lambda i, j: (i, j)),
        compiler_params=pltpu.CompilerParams(
            dimension_semantics=("parallel", "parallel"), vmem_limit_bytes=VMEM_LIMIT),
        name="inproj",
    )(a, b)


def _fill_ext(ext_ref, prev_ref, cur_ref, first):
    prev = prev_ref[...]
    ext_ref[0:HALO, :] = jnp.where(first, jnp.zeros_like(prev), prev)
    ext_ref[HALO:, :] = cur_ref[...]


def _conv_tile(ext_ref, w_ref, col, width, row0=0):
    acc = None
    for k in range(CONV_K):
        start = row0 + HALO - (CONV_K - 1) + k
        term = w_ref[k:k + 1, col:col + width] * ext_ref[start:start + CHUNK, col:col + width]
        acc = term if acc is None else acc + term
    return acc


def _gdn_kernel(q_ref, qprev_ref, z_ref, sm_ref, convw_ref, alog_ref, dtb_ref, normw_ref,
                o_ref, state_ref, ext_ref):
    c = pl.program_id(0)
    first = c == 0
    nb = q_ref.shape[0]
    nsub = q_ref.shape[1] // CHUNK
    dk = GDN_HEAD_DIM

    @pl.when(first)
    def _():
        state_ref[...] = jnp.zeros_like(state_ref)

    causal = _tri(CHUNK)
    strict = _tri(CHUNK, strict=True)
    ltri = causal.astype(BF16)
    eye = (lax.broadcasted_iota(jnp.int32, (CHUNK, CHUNK), 0)
           == lax.broadcasted_iota(jnp.int32, (CHUNK, CHUNK), 1)).astype(F32)
    scale = dk ** -0.5

    for b in range(nb):
        _fill_ext(ext_ref.at[b], qprev_ref.at[b], q_ref.at[b], first)

    def per_seq(b, j):
        sm = sm_ref[b, j * CHUNK:(j + 1) * CHUNK, :]
        g_all = -jnp.exp(alog_ref[...]) * _softplus(sm + dtb_ref[...])
        gc_all = _cumsum_rows(ltri, g_all)
        glast_all = gc_all[CHUNK - 1:CHUNK, :]
        return dict(beta=_sigmoid(sm), gc=gc_all, gc_t=gc_all.T, eg=jnp.exp(gc_all),
                    etail=jnp.exp(glast_all - gc_all), eglast=jnp.exp(glast_all))

    seqs = {(b, j): per_seq(b, j) for b in range(nb) for j in range(nsub)}

    items = [(j, b, h) for j in range(nsub) for b in range(nb) for h in range(GDN_HEADS)]
    idx = range(len(items))
    n_chain = nb * GDN_HEADS

    def prep(j, b, h):
        lo = h * dk
        sq = seqs[b, j]
        ext_b = ext_ref.at[b]
        q = _silu(_conv_tile(ext_b, convw_ref, lo, dk, j * CHUNK))
        k = _silu(_conv_tile(ext_b, convw_ref, GDN_WIDTH + lo, dk, j * CHUNK))
        v = _silu(_conv_tile(ext_b, convw_ref, 2 * GDN_WIDTH + lo, dk, j * CHUNK))
        q = q * lax.rsqrt(jnp.sum(q * q, axis=-1, keepdims=True) + NORM_EPS) * scale
        k = k * lax.rsqrt(jnp.sum(k * k, axis=-1, keepdims=True) + NORM_EPS)
        beta = sq["beta"][:, LANE_BETA + h:LANE_BETA + h + 1]
        ld = LANE_DECAY + h
        eg = sq["eg"][:, ld:ld + 1]
        decay = jnp.exp(jnp.where(causal, sq["gc"][:, ld:ld + 1] - sq["gc_t"][ld:ld + 1, :], -jnp.inf))
        kb = k * beta
        return dict(
            lhs_kk=jnp.concatenate([kb, q], axis=0).astype(BF16), k_b=k.astype(BF16), decay=decay,
            rhs_uw=jnp.concatenate([v * beta, kb * eg], axis=1).astype(BF16),
            q_dec=(q * eg).astype(BF16),
            k_tail_t=(k * sq["etail"][:, ld:ld + 1]).T.astype(BF16),
            eglast=sq["eglast"][:, ld:ld + 1])

    hd = [prep(j, b, h) for j, b, h in items]
    kk = [_dot_nt(d["lhs_kk"], d["k_b"]) for d in hd]
    qk = [(kk[i][CHUNK:] * hd[i]["decay"]).astype(BF16) for i in idx]

    b_pow = [-jnp.where(strict, kk[i][:CHUNK] * hd[i]["decay"], 0.0) for i in idx]
    x_inv = [eye + b for b in b_pow]
    b_pow = [_dot_b(b, b) for b in b_pow]
    n_sq = int(math.log2(CHUNK)) - 1
    for it in range(n_sq):
        if it < n_sq - 1:
            p = [_dot_b(jnp.concatenate([b_pow[i], x_inv[i]], axis=0), b_pow[i]) for i in idx]
            b_pow = [p[i][:CHUNK] for i in idx]
            x_inv = [x_inv[i] + p[i][CHUNK:] for i in idx]
        else:
            x_inv = [x_inv[i] + _dot_b(x_inv[i], b_pow[i]) for i in idx]

    uw = [_dot(x_inv[i].astype(BF16), hd[i]["rhs_uw"]) for i in idx]
    state = [state_ref[n] for n in range(n_chain)]
    out = [None] * len(items)
    for j in range(nsub):
        sub = range(j * n_chain, (j + 1) * n_chain)
        ws_qs = {i: _dot(jnp.concatenate([uw[i][:, dk:].astype(BF16), hd[i]["q_dec"]], axis=0),
                         state[i - j * n_chain].astype(BF16)) for i in sub}
        v_new = {i: (uw[i][:, :dk] - ws_qs[i][:CHUNK]).astype(BF16) for i in sub}
        for i in sub:
            out[i] = ws_qs[i][CHUNK:] + _dot(qk[i], v_new[i])
        state = [state[i - j * n_chain] * hd[i]["eglast"] + _dot(hd[i]["k_tail_t"], v_new[i]) for i in sub]
    for n in range(n_chain):
        state_ref[n] = state[n]
    for i, (j, b, h) in enumerate(items):
        lo = h * dk
        rows = slice(j * CHUNK, (j + 1) * CHUNK)
        o = out[i] * lax.rsqrt(jnp.mean(out[i] * out[i], axis=-1, keepdims=True) + NORM_EPS) * normw_ref[...]
        o_ref[b, rows, lo:lo + dk] = (o * _silu(z_ref[b, rows, lo:lo + dk])).astype(o_ref.dtype)


def _lane_vec(values, lane0):
    v = jnp.zeros((1, LANES), F32)
    return lax.dynamic_update_slice(v, values.astype(F32).reshape(1, -1), (0, lane0))


def _gdn(proj, small, conv_w, a_log, dt_bias, norm_w):
    bsz, seqlen, _ = proj.shape
    nc = seqlen // CHUNK
    wq = 3 * GDN_WIDTH
    nsub = GDN_CHUNKS_PER_STEP if nc % GDN_CHUNKS_PER_STEP == 0 else 1
    tb = nsub * CHUNK
    rows_per_halo = tb // HALO
    return pl.pallas_call(
        _gdn_kernel,
        out_shape=jax.ShapeDtypeStruct((bsz, seqlen, GDN_WIDTH), F32),
        grid=(nc // nsub,),
        in_specs=[
            pl.BlockSpec((bsz, tb, wq), lambda c: (0, c, OFF_QKV // wq)),
            pl.BlockSpec((bsz, HALO, wq),
                         lambda c: (0, jnp.maximum(c * rows_per_halo - 1, 0), OFF_QKV // wq)),
            pl.BlockSpec((bsz, tb, GDN_WIDTH), lambda c: (0, c, OFF_AZ // GDN_WIDTH)),
            pl.BlockSpec((bsz, tb, LANES), lambda c: (0, c, 0)),
            pl.BlockSpec((CONV_K, wq), lambda c: (0, 0)),
            pl.BlockSpec((1, LANES), lambda c: (0, 0)),
            pl.BlockSpec((1, LANES), lambda c: (0, 0)),
            pl.BlockSpec((1, GDN_HEAD_DIM), lambda c: (0, 0)),
        ],
        out_specs=pl.BlockSpec((bsz, tb, GDN_WIDTH), lambda c: (0, c, 0)),
        scratch_shapes=[pltpu.VMEM((bsz * GDN_HEADS, GDN_HEAD_DIM, GDN_HEAD_DIM), F32),
                        pltpu.VMEM((bsz, HALO + tb, wq), F32)],
        compiler_params=pltpu.CompilerParams(
            dimension_semantics=("arbitrary",), vmem_limit_bytes=VMEM_LIMIT),
        name="gdn",
    )(proj, proj, proj, small, conv_w.astype(F32), _lane_vec(a_log, LANE_DECAY),
      _lane_vec(dt_bias, LANE_DECAY), norm_w.astype(F32).reshape(1, GDN_HEAD_DIM))


def _ssd_kernel(x_ref, bc_ref, xprev_ref, bcprev_ref, z_ref, sm_ref, convw_ref, convb_ref,
                alog_ref, dtb_ref, dskip_ref, normw_ref, expand_ref,
                o_ref, state_ref, ext_ref):
    c = pl.program_id(0)
    first = c == 0
    nb = x_ref.shape[0]

    @pl.when(first)
    def _():
        state_ref[...] = jnp.zeros_like(state_ref)

    causal = _tri(CHUNK)
    ltri = causal.astype(BF16)
    expand = expand_ref[...]

    def per_seq(b):
        prev_x = xprev_ref[b]
        prev_bc = bcprev_ref[b]
        ext_ref[b, 0:HALO, 0:M2_WIDTH] = jnp.where(first, jnp.zeros_like(prev_x), prev_x)
        ext_ref[b, 0:HALO, M2_WIDTH:] = jnp.where(first, jnp.zeros_like(prev_bc), prev_bc)
        ext_ref[b, HALO:, 0:M2_WIDTH] = x_ref[b]
        ext_ref[b, HALO:, M2_WIDTH:] = bc_ref[b]
        dt_all = _softplus(sm_ref[b] + dtb_ref[...])
        a_all = -jnp.exp(alog_ref[...]) * dt_all
        acum = _cumsum_rows(ltri, a_all)
        ea = jnp.exp(acum)
        ds = jnp.exp(acum[CHUNK - 1:CHUNK, :] - acum)
        sp_hi, sp_lo = _split_bf16(jnp.concatenate([dt_all, ea, ds], axis=0), 2)
        spread = _dot(jnp.concatenate([sp_hi, sp_lo], axis=0), expand)
        spread = spread[:3 * CHUNK] + spread[3 * CHUNK:]
        ea_x = spread[CHUNK:2 * CHUNK]
        return dict(acum=acum, acum_t=acum.T, dt_x=spread[:CHUNK], ea_x=ea_x, ds_x=spread[2 * CHUNK:],
                    cdecay_x=ea_x[CHUNK - 1:CHUNK, :])

    seqs = [per_seq(b) for b in range(nb)]

    lane = lax.broadcasted_iota(jnp.int32, (CHUNK, 2 * M2_HEAD_DIM), 1)
    heads_per_group = M2_HEADS // M2_GROUPS

    items = [(b, g) for b in range(nb) for g in range(M2_GROUPS)]
    idx = range(len(items))

    def prep(b, g):
        glo = g * M2_GROUP_W
        bcol = M2_WIDTH + g * M2_STATE
        ccol = M2_WIDTH + M2_BC + g * M2_STATE
        ext_b = ext_ref.at[b]
        bm = _silu(_conv_tile(ext_b, convw_ref, bcol, M2_STATE) + convb_ref[:, bcol:bcol + M2_STATE])
        cm = _silu(_conv_tile(ext_b, convw_ref, ccol, M2_STATE) + convb_ref[:, ccol:ccol + M2_STATE])
        xs = _silu(_conv_tile(ext_b, convw_ref, glo, M2_GROUP_W) + convb_ref[:, glo:glo + M2_GROUP_W])
        xdt = xs * seqs[b]["dt_x"][:, glo:glo + M2_GROUP_W]
        return dict(xs=xs, xdt_b=xdt.astype(BF16),
                    xds=(xdt * seqs[b]["ds_x"][:, glo:glo + M2_GROUP_W]).astype(BF16),
                    bm_b=bm.astype(BF16), bm_t=bm.T.astype(BF16), cm_b=cm.astype(BF16))

    gd = [prep(b, g) for b, g in items]
    scores = [_dot_nt(d["cm_b"], d["bm_b"]) for d in gd]
    s_prev = [state_ref[i] for i in idx]
    y_off = [_dot(gd[i]["cm_b"], s_prev[i].astype(BF16)) for i in idx]
    states = [_dot(d["bm_t"], d["xds"]) for d in gd]
    for i, (b, g) in enumerate(items):
        glo = g * M2_GROUP_W
        state_ref[i] = s_prev[i] * seqs[b]["cdecay_x"][:, glo:glo + M2_GROUP_W] + states[i]

    def diag(i, j, r):
        b, g = items[i]
        la = LANE_DT + g * heads_per_group + 2 * j + r
        seg = jnp.exp(jnp.where(causal, seqs[b]["acum"][:, la:la + 1] - seqs[b]["acum_t"][la:la + 1, :],
                                -jnp.inf))
        x2 = gd[i]["xdt_b"][:, j * 2 * M2_HEAD_DIM:(j + 1) * 2 * M2_HEAD_DIM]
        return _dot((scores[i] * seg).astype(BF16), x2)

    pairs = range(heads_per_group // 2)
    yd = [[[diag(i, j, r) for r in range(2)] for j in pairs] for i in idx]
    for i, (b, g) in enumerate(items):
        glo = g * M2_GROUP_W
        parts = [jnp.where(lane < M2_HEAD_DIM, yd[i][j][0], yd[i][j][1]) for j in pairs]
        y = jnp.concatenate(parts, axis=1) + y_off[i] * seqs[b]["ea_x"][:, glo:glo + M2_GROUP_W]
        y = y + dskip_ref[:, glo:glo + M2_GROUP_W] * gd[i]["xs"]
        y = y * _silu(z_ref[b, :, glo:glo + M2_GROUP_W])
        y = y * lax.rsqrt(jnp.mean(y * y, axis=-1, keepdims=True) + NORM_EPS)
        o_ref[b, :, glo:glo + M2_GROUP_W] = (y * normw_ref[:, glo:glo + M2_GROUP_W]).astype(o_ref.dtype)


def _ssd(proj, small, conv_w, conv_b, a_log, dt_bias, d_skip, norm_w):
    bsz, seqlen, _ = proj.shape
    nc = seqlen // CHUNK
    rows_per_halo = CHUNK // HALO
    conv_dim = M2_WIDTH + 2 * M2_BC
    head_of_col = jnp.arange(M2_WIDTH) // M2_HEAD_DIM
    expand = (jnp.arange(LANES)[:, None] == (LANE_DT + head_of_col)[None, :]).astype(BF16)
    dskip_x = jnp.repeat(d_skip.astype(F32), M2_HEAD_DIM).reshape(1, M2_WIDTH)
    xi = OFF_XBC // M2_WIDTH

    def halo_idx(c):
        return jnp.maximum(c * rows_per_halo - 1, 0)

    return pl.pallas_call(
        _ssd_kernel,
        out_shape=jax.ShapeDtypeStruct((bsz, seqlen, M2_WIDTH), F32),
        grid=(nc,),
        in_specs=[
            pl.BlockSpec((bsz, CHUNK, M2_WIDTH), lambda c: (0, c, xi)),
            pl.BlockSpec((bsz, CHUNK, 2 * M2_BC), lambda c: (0, c, xi + 1)),
            pl.BlockSpec((bsz, HALO, M2_WIDTH), lambda c: (0, halo_idx(c), xi)),
            pl.BlockSpec((bsz, HALO, 2 * M2_BC), lambda c: (0, halo_idx(c), xi + 1)),
            pl.BlockSpec((bsz, CHUNK, M2_WIDTH), lambda c: (0, c, OFF_CZ // M2_WIDTH)),
            pl.BlockSpec((bsz, CHUNK, LANES), lambda c: (0, c, 0)),
            pl.BlockSpec((CONV_K, conv_dim), lambda c: (0, 0)),
            pl.BlockSpec((1, conv_dim), lambda c: (0, 0)),
            pl.BlockSpec((1, LANES), lambda c: (0, 0)),
            pl.BlockSpec((1, LANES), lambda c: (0, 0)),
            pl.BlockSpec((1, M2_WIDTH), lambda c: (0, 0)),
            pl.BlockSpec((1, M2_WIDTH), lambda c: (0, 0)),
            pl.BlockSpec((LANES, M2_WIDTH), lambda c: (0, 0)),
        ],
        out_specs=pl.BlockSpec((bsz, CHUNK, M2_WIDTH), lambda c: (0, c, 0)),
        scratch_shapes=[pltpu.VMEM((bsz * M2_GROUPS, M2_STATE, M2_GROUP_W), F32),
                        pltpu.VMEM((bsz, HALO + CHUNK, conv_dim), F32)],
        compiler_params=pltpu.CompilerParams(
            dimension_semantics=("arbitrary",), vmem_limit_bytes=VMEM_LIMIT),
        name="ssd",
    )(proj, proj, proj, proj, proj, small, conv_w.astype(F32), conv_b.astype(F32).reshape(1, conv_dim),
      _lane_vec(a_log, LANE_DT), _lane_vec(dt_bias, LANE_DT), dskip_x,
      norm_w.astype(F32).reshape(1, M2_WIDTH), expand)


def _s5_kernel(u_ref, lamre_c_ref, lamim_c_ref, lamre_r_ref, lamim_r_ref, lstep_ref,
               bre_c_ref, bim_c_ref, bre_r_ref, bim_r_ref, cre_c_ref, cim_c_ref,
               o_ref, toep_ref, *, chunks_per_seq):
    p = S5_STATE
    hs = S5_GROUP_SIZE
    cw = CHUNK * hs
    step = jnp.exp(lstep_ref[0:1, :])
    step_r = step[:, :p]

    def disc(lre, lim, st):
        lre = jnp.minimum(lre, -1e-4)
        mag = jnp.exp(lre * st)
        are = mag * jnp.cos(lim * st)
        aim = mag * jnp.sin(lim * st)
        den = lre * lre + lim * lim
        fre = ((are - 1.0) * lre + aim * lim) / den
        fim = (aim * lre - (are - 1.0) * lim) / den
        return lre, are, aim, fre, fim

    lre_c, are_c, aim_c, fre_c, fim_c = disc(lamre_c_ref[...], lamim_c_ref[...], step)
    lim_c = lamim_c_ref[...]
    bbre_c = fre_c * bre_c_ref[...] - fim_c * bim_c_ref[...]
    bbim_c = fre_c * bim_c_ref[...] + fim_c * bre_c_ref[...]
    lre_r, _, _, fre_r, fim_r = disc(lamre_r_ref[0:1, :], lamim_r_ref[0:1, :], step_r)
    lim_r = lamim_r_ref[0:1, :]
    bbre_r = fre_r * bre_r_ref[...] - fim_r * bim_r_ref[...]
    bbim_r = fre_r * bim_r_ref[...] + fim_r * bre_r_ref[...]

    per = LANES // hs
    ntile = cw // LANES
    r_lane = lax.shift_right_logical(lax.broadcasted_iota(jnp.int32, (p, LANES), 1),
                                     int(math.log2(hs))).astype(F32)

    def cmul(a, b):
        return a[0] * b[0] - a[1] * b[1], a[0] * b[1] + a[1] * b[0]

    def cpow(expo):
        mag = jnp.exp(lre_c * step * expo)
        ang = lim_c * step * expo
        return mag * jnp.cos(ang), mag * jnp.sin(ang)

    a_c = (are_c, aim_c)
    c_fwd = cmul((cre_c_ref[...], cim_c_ref[...]), cpow(r_lane))
    c_fwd1 = cmul(c_fwd, a_c)
    b_rev = cmul((bbre_c, bbim_c), cpow(float(per - 1) - r_lane))
    a_per = cpow(float(per))
    tile_pow = [(jnp.ones_like(are_c), jnp.zeros_like(are_c))]
    for _ in range(ntile - 1):
        tile_pow.append(cmul(tile_pow[-1], a_per))
    m0 = [cmul(c_fwd, tile_pow[k]) for k in range(ntile)]
    m1 = [cmul(c_fwd1, tile_pow[k]) for k in range(ntile)]
    nn = [cmul(b_rev, tile_pow[ntile - 1 - k]) for k in range(ntile)]
    m0re = jnp.concatenate([t[0] for t in m0], axis=1)
    m0im = jnp.concatenate([t[1] for t in m0], axis=1)
    m1re = jnp.concatenate([t[0] for t in m1], axis=1)
    m1im = jnp.concatenate([t[1] for t in m1], axis=1)
    nre = jnp.concatenate([t[0] for t in nn], axis=1)
    nim = jnp.concatenate([t[1] for t in nn], axis=1)

    kt = _dot_hi(bbre_r, m0re) - _dot_hi(bbim_r, m0im)
    lane_k = lax.broadcasted_iota(jnp.int32, (hs, cw), 1)

    for r in range(per):
        kr = pltpu.roll(kt, r * hs, 1) if r else kt
        for q in range(ntile):
            s = q * per + r
            cut = cw - q * LANES
            rolled = kr if q == 0 else jnp.concatenate([kr[:, cut:], kr[:, :cut]], axis=1)
            toep_ref[s * hs:(s + 1) * hs, :] = jnp.where(lane_k >= s * hs, rolled, 0.0).astype(BF16)

    ub = u_ref[...]
    y = _dot(ub, toep_ref[...])
    sre = _dot_nt(ub, nre.astype(BF16))
    sim = _dot_nt(ub, nim.astype(BF16))

    rows = ub.shape[0]
    cidx = lax.broadcasted_iota(jnp.int32, (rows, p), 0) & (chunks_per_seq - 1)
    mag_c = jnp.exp(lre_r * step_r * float(CHUNK))
    pre = mag_c * jnp.cos(lim_r * step_r * float(CHUNK))
    pim = mag_c * jnp.sin(lim_r * step_r * float(CHUNK))
    d = 1
    while d < chunks_per_seq:
        ok = cidx >= d
        shre = jnp.where(ok, pltpu.roll(sre, d, 0), 0.0)
        shim = jnp.where(ok, pltpu.roll(sim, d, 0), 0.0)
        sre, sim = sre + pre * shre - pim * shim, sim + pre * shim + pim * shre
        pre, pim = pre * pre - pim * pim, 2.0 * pre * pim
        d *= 2
    ok = cidx >= 1
    xsre = jnp.where(ok, pltpu.roll(sre, 1, 0), 0.0)
    xsim = jnp.where(ok, pltpu.roll(sim, 1, 0), 0.0)
    y = y + _dot(xsre.astype(BF16), m1re.astype(BF16)) - _dot(xsim.astype(BF16), m1im.astype(BF16))
    o_ref[...] = y.astype(o_ref.dtype)


def _s5(u3, lam_re, lam_im, log_step, b_re, b_im, c_re, c_im, chunks_per_seq):
    g, rows, cw = u3.shape
    p, hs = S5_STATE, S5_GROUP_SIZE
    rep = LANES // hs
    f32 = lambda a: a.astype(F32)
    col = lambda a: jnp.broadcast_to(f32(a)[:, :, None], (g, p, LANES))
    row = lambda a: jnp.broadcast_to(f32(a)[:, None, :], (g, HALO, p))
    lstep = jnp.broadcast_to(f32(log_step)[:, None, None], (g, HALO, LANES))
    b_c = lambda a: jnp.tile(f32(a), (1, 1, rep))
    b_r = lambda a: jnp.swapaxes(f32(a), 1, 2)
    c_c = lambda a: jnp.tile(jnp.swapaxes(f32(a), 1, 2), (1, 1, rep))

    def gspec(shape):
        return pl.BlockSpec((None,) + shape, lambda i: (i, 0, 0))

    return pl.pallas_call(
        functools.partial(_s5_kernel, chunks_per_seq=chunks_per_seq),
        out_shape=jax.ShapeDtypeStruct((g, rows, cw), BF16),
        grid=(g,),
        in_specs=[gspec((rows, cw)),
                  gspec((p, LANES)), gspec((p, LANES)), gspec((HALO, p)), gspec((HALO, p)),
                  gspec((HALO, LANES)),
                  gspec((p, LANES)), gspec((p, LANES)), gspec((hs, p)), gspec((hs, p)),
                  gspec((p, LANES)), gspec((p, LANES))],
        out_specs=gspec((rows, cw)),
        scratch_shapes=[pltpu.VMEM((cw, cw), BF16)],
        compiler_params=pltpu.CompilerParams(
            dimension_semantics=("parallel",), vmem_limit_bytes=VMEM_LIMIT),
        name="s5",
    )(u3, col(lam_re), col(lam_im), row(lam_re), row(lam_im), lstep,
      b_c(b_re), b_c(b_im), b_r(b_re), b_r(b_im), c_c(c_re), c_c(c_im))


def _merge_kernel(x_ref, gate_ref, ya_ref, ys_ref, su0_ref, su1_ref, su2_ref, sg0_ref, sg1_ref, sg2_ref, yc_ref,
                  sd_ref, gluw_ref, glub_ref, pa_ref, pb_ref, pc_ref, wout_ref, nw_ref, o_ref, *h_ref, final):
    d = x_ref.shape[-1]
    su = jnp.concatenate([su0_ref[...], su1_ref[...], su2_ref[...]], axis=1)
    ys = jax.nn.gelu(ys_ref[...].astype(F32) + sd_ref[...] * su)
    glu = _sigmoid(_dot(ys.astype(BF16), gluw_ref[...]) + glub_ref[...])
    sgate = jnp.concatenate([sg0_ref[...], sg1_ref[...], sg2_ref[...]], axis=1)
    yb = ys * glu * _silu(sgate)
    pa = _dot(ya_ref[...].astype(BF16), pa_ref[...])
    pb = _dot(yb.astype(BF16), pb_ref[...])
    pc = _dot(yc_ref[...].astype(BF16), pc_ref[...])
    merged = (_sigmoid(gate_ref[:, 0:d]) * pa
              + _sigmoid(gate_ref[:, d:2 * d]) * pb
              + _sigmoid(gate_ref[:, 2 * d:3 * d]) * pc)
    out = x_ref[...] + _dot(merged.astype(BF16), wout_ref[...])
    normed = out * lax.rsqrt(jnp.mean(out * out, axis=-1, keepdims=True) + NORM_EPS) * nw_ref[...]
    if final:
        o_ref[...] = normed.astype(o_ref.dtype)
    else:
        o_ref[...] = out.astype(o_ref.dtype)
        h_ref[0][...] = normed.astype(h_ref[0].dtype)


def _merge(x2d, proj2d, ya, ys, yc, s5_d, glu_w, glu_b, proj_a, proj_b, proj_c, w_out, next_norm_w, final, tm=256):
    t, d = x2d.shape
    tm = min(tm, t)
    sgw = S5_WIDTH // 3
    assert OFF_SG % sgw == 0 and OFF_SU % sgw == 0 and sgw % LANES == 0

    def wspec(shape):
        return pl.BlockSpec(shape, lambda i: (0, 0), pipeline_mode=pl.Buffered(1))

    row_spec = pl.BlockSpec((tm, d), lambda i: (i, 0))
    if final:
        out_shape, out_specs = jax.ShapeDtypeStruct((t, d), F32), row_spec
    else:
        out_shape = (jax.ShapeDtypeStruct((t, d), F32), jax.ShapeDtypeStruct((t, d), BF16))
        out_specs = (row_spec, row_spec)

    return pl.pallas_call(
        functools.partial(_merge_kernel, final=final),
        out_shape=out_shape,
        grid=(t // tm,),
        in_specs=[
            pl.BlockSpec((tm, d), lambda i: (i, 0)),
            pl.BlockSpec((tm, 3 * d), lambda i: (i, OFF_MERGE // (3 * d))),
            pl.BlockSpec((tm, GDN_WIDTH), lambda i: (i, 0)),
            pl.BlockSpec((tm, S5_WIDTH), lambda i: (i, 0)),
            pl.BlockSpec((tm, sgw), lambda i: (i, OFF_SU // sgw)),
            pl.BlockSpec((tm, sgw), lambda i: (i, OFF_SU // sgw + 1)),
            pl.BlockSpec((tm, sgw), lambda i: (i, OFF_SU // sgw + 2)),
            pl.BlockSpec((tm, sgw), lambda i: (i, OFF_SG // sgw)),
            pl.BlockSpec((tm, sgw), lambda i: (i, OFF_SG // sgw + 1)),
            pl.BlockSpec((tm, sgw), lambda i: (i, OFF_SG // sgw + 2)),
            pl.BlockSpec((tm, M2_WIDTH), lambda i: (i, 0)),
            wspec((1, S5_WIDTH)),
            wspec((S5_WIDTH, S5_WIDTH)), wspec((1, S5_WIDTH)),
            wspec((GDN_WIDTH, d)), wspec((S5_WIDTH, d)), wspec((M2_WIDTH, d)), wspec((d, d)),
            wspec((1, d)),
        ],
        out_specs=out_specs,
        compiler_params=pltpu.CompilerParams(
            dimension_semantics=("parallel",), vmem_limit_bytes=VMEM_LIMIT),
        name="merge",
    )(x2d, proj2d, ya, ys, proj2d, proj2d, proj2d, proj2d, proj2d, proj2d, yc,
      s5_d.astype(F32).reshape(1, S5_WIDTH), glu_w.astype(BF16), glu_b.astype(F32).reshape(1, -1),
      proj_a.astype(BF16), proj_b.astype(BF16), proj_c.astype(BF16), w_out.astype(BF16),
      next_norm_w.astype(F32).reshape(1, d))


def _w_in_segments(d_model):
    conv_dim = M2_WIDTH + 2 * M2_BC
    o = 0
    seg = {}
    for name, width in (("qkv", 3 * GDN_WIDTH), ("az", GDN_WIDTH), ("beta", GDN_HEADS), ("decay", GDN_HEADS),
                        ("su", S5_WIDTH), ("sg", S5_WIDTH), ("cz", M2_WIDTH), ("xbc", conv_dim),
                        ("dt", M2_HEADS), ("merge", 3 * d_model)):
        seg[name] = (o, width)
        o += width
    return seg


_MAIN_ORDER = (("merge", OFF_MERGE), ("qkv", OFF_QKV), ("xbc", OFF_XBC), ("az", OFF_AZ), ("cz", OFF_CZ),
               ("su", OFF_SU), ("sg", OFF_SG))


def _inproj_kernel(src_ref, h_ref, wt_ref, o_ref):
    del src_ref
    o_ref[...] = _dot_nt(h_ref[...], wt_ref[...].astype(BF16)).astype(o_ref.dtype)


def _inproj(h, wt_all, layer):
    m, k = h.shape
    seg = _w_in_segments(k)
    tm, tn = min(INPROJ_TM, m), INPROJ_TN

    def src_of(col):
        name, dst = [(nm, d0) for nm, d0 in _MAIN_ORDER if d0 <= col < d0 + seg[nm][1]][0]
        return seg[name][0] + col - dst

    src_rows = [src_of(j * tn) for j in range(MAIN_W // tn)]
    assert all(src_of(j * tn + tn - 1) == s + tn - 1 for j, s in enumerate(src_rows))
    return pl.pallas_call(
        _inproj_kernel,
        out_shape=jax.ShapeDtypeStruct((m, MAIN_W), F32),
        grid_spec=pltpu.PrefetchScalarGridSpec(
            num_scalar_prefetch=1,
            grid=(m // tm, MAIN_W // tn),
            in_specs=[pl.BlockSpec((tm, k), lambda i, j, src: (i, 0)),
                      pl.BlockSpec((None, pl.Element(tn), pl.Element(k)),
                                   lambda i, j, src: (layer, pl.multiple_of(src[j], HALO), 0))],
            out_specs=pl.BlockSpec((tm, tn), lambda i, j, src: (i, j))),
        compiler_params=pltpu.CompilerParams(
            dimension_semantics=("parallel", "arbitrary"), vmem_limit_bytes=VMEM_LIMIT),
        name="inproj",
    )(jnp.asarray(src_rows, jnp.int32), h, wt_all)


def _pack_small_kernel(gdn_ref, dt_ref, o_ref):
    o_ref[...] = jnp.zeros_like(o_ref)
    o_ref[LANE_BETA:LANE_BETA + 2 * GDN_HEADS, :] = gdn_ref[...].astype(o_ref.dtype)
    o_ref[LANE_DT:LANE_DT + M2_HEADS, :] = dt_ref[...].astype(o_ref.dtype)


def _pack_small(wt_all, layer):
    _, n, k = wt_all.shape
    seg = _w_in_segments(k)
    n_gdn = 2 * GDN_HEADS
    assert seg["decay"][0] == seg["beta"][0] + GDN_HEADS
    return pl.pallas_call(
        _pack_small_kernel,
        out_shape=jax.ShapeDtypeStruct((LANES, k), BF16),
        grid=(1,),
        in_specs=[pl.BlockSpec((None, pl.Element(n_gdn), pl.Element(k)), lambda i: (layer, seg["beta"][0], 0)),
                  pl.BlockSpec((None, pl.Element(M2_HEADS), pl.Element(k)), lambda i: (layer, seg["dt"][0], 0))],
        out_specs=pl.BlockSpec((LANES, k), lambda i: (0, 0)),
        name="packs",
    )(wt_all, wt_all)


def _layer(x2d, h, bsz, seqlen, w_in_all, layer, gdn_conv_w, gdn_a_log, gdn_dt_bias, gdn_norm_w,
           s5_lam_re, s5_lam_im, s5_log_step, s5_b_re, s5_b_im, s5_c_re, s5_c_im, s5_d,
           s5_glu_w, s5_glu_b, m2_conv_w, m2_conv_b, m2_a_log, m2_dt_bias, m2_d, m2_norm_w,
           proj_a, proj_b, proj_c, w_out, next_norm_w, final):
    t = bsz * seqlen
    nc = seqlen // CHUNK
    proj = _inproj(h, w_in_all, layer)
    small = _matmul(h, _pack_small(w_in_all, layer), INPROJ_TM, LANES, rhs_transposed=True)
    proj3 = proj.reshape(bsz, seqlen, MAIN_W)
    small3 = small.reshape(bsz, seqlen, LANES)

    ya = _gdn(proj3, small3, gdn_conv_w, gdn_a_log, gdn_dt_bias, gdn_norm_w)
    yc = _ssd(proj3, small3, m2_conv_w, m2_conv_b, m2_a_log, m2_dt_bias, m2_d, m2_norm_w)

    u = proj[:, OFF_SU:OFF_SU + S5_WIDTH].astype(BF16).reshape(bsz * nc, CHUNK, S5_GROUPS, S5_GROUP_SIZE)
    u = jnp.transpose(u, (2, 0, 1, 3)).reshape(S5_GROUPS, bsz * nc, CHUNK * S5_GROUP_SIZE)
    ys = _s5(u, s5_lam_re, s5_lam_im, s5_log_step, s5_b_re, s5_b_im, s5_c_re, s5_c_im, nc)
    ys = ys.reshape(S5_GROUPS, bsz * nc, CHUNK, S5_GROUP_SIZE)
    ys = jnp.transpose(ys, (1, 2, 0, 3)).reshape(t, S5_WIDTH)

    return _merge(x2d, proj, ya.reshape(t, GDN_WIDTH), ys, yc.reshape(t, M2_WIDTH), s5_d,
                  s5_glu_w, s5_glu_b, proj_a, proj_b, proj_c, w_out, next_norm_w, final)


def kernel(x, norm_w, w_in, gdn_conv_w, gdn_a_log, gdn_dt_bias, gdn_norm_w, s5_lam_re, s5_lam_im, s5_log_step, s5_b_re, s5_b_im, s5_c_re, s5_c_im, s5_d, s5_glu_w, s5_glu_b, m2_conv_w, m2_conv_b, m2_a_log, m2_dt_bias, m2_d, m2_norm_w, proj_a, proj_b, proj_c, w_out, final_norm_w):
    bsz, seqlen, d = x.shape
    depth = norm_w.shape[0]
    x2d = x.reshape(bsz * seqlen, d)
    h = _rmsnorm(x2d, norm_w[0], BF16)
    w_in = jnp.swapaxes(w_in, 1, 2)
    for i in range(depth):
        final = i == depth - 1
        res = _layer(x2d, h, bsz, seqlen, w_in, i, gdn_conv_w[i], gdn_a_log[i], gdn_dt_bias[i],
                     gdn_norm_w[i], s5_lam_re[i], s5_lam_im[i], s5_log_step[i], s5_b_re[i], s5_b_im[i],
                     s5_c_re[i], s5_c_im[i], s5_d[i], s5_glu_w[i], s5_glu_b[i],
                     m2_conv_w[i], m2_conv_b[i], m2_a_log[i], m2_dt_bias[i], m2_d[i], m2_norm_w[i],
                     proj_a[i], proj_b[i], proj_c[i], w_out[i],
                     final_norm_w if final else norm_w[i + 1], final)
        if final:
            return res.reshape(bsz, seqlen, d)
        x2d, h = res
```

```python
import functools
import math

import jax
import jax.numpy as jnp
from jax import lax
from jax.experimental import pallas as pl
from jax.experimental.pallas import tpu as pltpu

F32 = jnp.float32
BF16 = jnp.bfloat16
HIGHEST = lax.Precision.HIGHEST

NORM_EPS = 1e-6
CHUNK = 64
CONV_K = 4
HALO = 8
LANES = 128

GDN_HEADS = 8
GDN_HEAD_DIM = 128
GDN_WIDTH = GDN_HEADS * GDN_HEAD_DIM
S5_GROUP_SIZE = 16
S5_GROUPS = 48
S5_STATE = 64
S5_WIDTH = S5_GROUPS * S5_GROUP_SIZE
M2_HEADS = 16
M2_HEAD_DIM = 64
M2_WIDTH = M2_HEADS * M2_HEAD_DIM
M2_GROUPS = 4
M2_STATE = 128
M2_BC = M2_GROUPS * M2_STATE
M2_GROUP_W = M2_WIDTH // M2_GROUPS

OFF_MERGE = 0
OFF_QKV = OFF_MERGE + 3 * 2048
OFF_XBC = OFF_QKV + 3 * GDN_WIDTH
OFF_AZ = OFF_XBC + M2_WIDTH + 2 * M2_BC
OFF_CZ = OFF_AZ + GDN_WIDTH
OFF_SU = OFF_CZ + M2_WIDTH
OFF_SG = OFF_SU + S5_WIDTH
MAIN_W = OFF_SG + S5_WIDTH
LANE_BETA = 0
LANE_DECAY = GDN_HEADS
LANE_DT = 2 * GDN_HEADS

VMEM_LIMIT = 56 * 1024 * 1024
INPROJ_TM = 2048
INPROJ_TN = 512
GDN_CHUNKS_PER_STEP = 1


def _softplus(x):
    return jnp.maximum(x, 0.0) + jnp.log(1.0 + jnp.exp(-jnp.abs(x)))


def _sigmoid(x):
    return 0.5 * jnp.tanh(0.5 * x) + 0.5


def _silu(x):
    h = 0.5 * x
    return h + h * jnp.tanh(h)


def _dot(a, b):
    return jnp.dot(a, b, preferred_element_type=F32)


def _dot_nt(a, b):
    return lax.dot_general(a, b, (((1,), (1,)), ((), ())), preferred_element_type=F32)


def _dot_hi(a, b):
    return jnp.dot(a, b, preferred_element_type=F32, precision=HIGHEST)


def _dot_b(a, b):
    return _dot(a.astype(BF16), b.astype(BF16))


def _split_bf16(x, terms):
    parts = []
    for _ in range(terms):
        p = x.astype(BF16)
        parts.append(p)
        x = x - p.astype(F32)
    return parts


def _cumsum_rows(ltri_b, x):
    n = x.shape[1]
    s = _dot(ltri_b, jnp.concatenate(_split_bf16(x, 3), axis=1))
    return s[:, :n] + s[:, n:2 * n] + s[:, 2 * n:]


def _tri(n, strict=False):
    r = lax.broadcasted_iota(jnp.int32, (n, n), 0)
    c = lax.broadcasted_iota(jnp.int32, (n, n), 1)
    return (r > c) if strict else (r >= c)


def _rmsnorm_kernel(x_ref, w_ref, o_ref):
    x = x_ref[...]
    y = x * lax.rsqrt(jnp.mean(x * x, axis=-1, keepdims=True) + NORM_EPS)
    o_ref[...] = (y * w_ref[...]).astype(o_ref.dtype)


def _rmsnorm(x2d, w, out_dtype, tm=512):
    t, d = x2d.shape
    tm = min(tm, t)
    return pl.pallas_call(
        _rmsnorm_kernel,
        out_shape=jax.ShapeDtypeStruct((t, d), out_dtype),
        grid=(t // tm,),
        in_specs=[pl.BlockSpec((tm, d), lambda i: (i, 0)),
                  pl.BlockSpec((1, d), lambda i: (0, 0))],
        out_specs=pl.BlockSpec((tm, d), lambda i: (i, 0)),
        compiler_params=pltpu.CompilerParams(dimension_semantics=("parallel",)),
        name="rmsnorm",
    )(x2d, w.reshape(1, d))


def _matmul_kernel(a_ref, b_ref, o_ref, *, rhs_transposed):
    dot = _dot_nt if rhs_transposed else _dot
    o_ref[...] = dot(a_ref[...], b_ref[...]).astype(o_ref.dtype)


def _matmul(a, b, tm, tn, out_dtype=F32, rhs_transposed=False):
    m, k = a.shape
    n = b.shape[0] if rhs_transposed else b.shape[1]
    tm = min(tm, m)
    tn = min(tn, n)
    b_spec = (pl.BlockSpec((tn, k), lambda i, j: (j, 0)) if rhs_transposed
              else pl.BlockSpec((k, tn), lambda i, j: (0, j)))
    return pl.pallas_call(
        functools.partial(_matmul_kernel, rhs_transposed=rhs_transposed),
        out_shape=jax.ShapeDtypeStruct((m, n), out_dtype),
        grid=(m // tm, n // tn),
        in_specs=[pl.BlockSpec((tm, k), lambda i, j: (i, 0)), b_spec],
        out_specs=pl.BlockSpec((tm, tn), lambda i, j: (i, j)),
        compiler_params=pltpu.CompilerParams(
            dimension_semantics=("parallel", "parallel"), vmem_limit_bytes=VMEM_LIMIT),
        name="inproj",
    )(a, b)


def _fill_ext(ext_ref, prev_ref, cur_ref, first):
    prev = prev_ref[...]
    ext_ref[0:HALO, :] = jnp.where(first, jnp.zeros_like(prev), prev)
    ext_ref[HALO:, :] = cur_ref[...]


def _conv_tile(ext_ref, w_ref, col, width, row0=0):
    acc = None
    for k in range(CONV_K):
        start = row0 + HALO - (CONV_K - 1) + k
        term = w_ref[k:k + 1, col:col + width] * ext_ref[start:start + CHUNK, col:col + width]
        acc = term if acc is None else acc + term
    return acc


def _gdn_kernel(q_ref, qprev_ref, z_ref, sm_ref, convw_ref, alog_ref, dtb_ref, normw_ref,
                o_ref, state_ref, ext_ref):
    c = pl.program_id(0)
    first = c == 0
    nb = q_ref.shape[0]
    nsub = q_ref.shape[1] // CHUNK
    dk = GDN_HEAD_DIM

    @pl.when(first)
    def _():
        state_ref[...] = jnp.zeros_like(state_ref)

    causal = _tri(CHUNK)
    strict = _tri(CHUNK, strict=True)
    ltri = causal.astype(BF16)
    eye = (lax.broadcasted_iota(jnp.int32, (CHUNK, CHUNK), 0)
           == lax.broadcasted_iota(jnp.int32, (CHUNK, CHUNK), 1)).astype(F32)
    scale = dk ** -0.5

    for b in range(nb):
        _fill_ext(ext_ref.at[b], qprev_ref.at[b], q_ref.at[b], first)

    def per_seq(b, j):
        sm = sm_ref[b, j * CHUNK:(j + 1) * CHUNK, :]
        g_all = -jnp.exp(alog_ref[...]) * _softplus(sm + dtb_ref[...])
        gc_all = _cumsum_rows(ltri, g_all)
        glast_all = gc_all[CHUNK - 1:CHUNK, :]
        return dict(beta=_sigmoid(sm), gc=gc_all, gc_t=gc_all.T, eg=jnp.exp(gc_all),
                    etail=jnp.exp(glast_all - gc_all), eglast=jnp.exp(glast_all))

    seqs = {(b, j): per_seq(b, j) for b in range(nb) for j in range(nsub)}

    items = [(j, b, h) for j in range(nsub) for b in range(nb) for h in range(GDN_HEADS)]
    idx = range(len(items))
    n_chain = nb * GDN_HEADS

    def prep(j, b, h):
        lo = h * dk
        sq = seqs[b, j]
        ext_b = ext_ref.at[b]
        q = _silu(_conv_tile(ext_b, convw_ref, lo, dk, j * CHUNK))
        k = _silu(_conv_tile(ext_b, convw_ref, GDN_WIDTH + lo, dk, j * CHUNK))
        v = _silu(_conv_tile(ext_b, convw_ref, 2 * GDN_WIDTH + lo, dk, j * CHUNK))
        q = q * lax.rsqrt(jnp.sum(q * q, axis=-1, keepdims=True) + NORM_EPS) * scale
        k = k * lax.rsqrt(jnp.sum(k * k, axis=-1, keepdims=True) + NORM_EPS)
        beta = sq["beta"][:, LANE_BETA + h:LANE_BETA + h + 1]
        ld = LANE_DECAY + h
        eg = sq["eg"][:, ld:ld + 1]
        decay = jnp.exp(jnp.where(causal, sq["gc"][:, ld:ld + 1] - sq["gc_t"][ld:ld + 1, :], -jnp.inf))
        kb = k * beta
        return dict(
            lhs_kk=jnp.concatenate([kb, q], axis=0).astype(BF16), k_b=k.astype(BF16), decay=decay,
            rhs_uw=jnp.concatenate([v * beta, kb * eg], axis=1).astype(BF16),
            q_dec=(q * eg).astype(BF16),
            k_tail_t=(k * sq["etail"][:, ld:ld + 1]).T.astype(BF16),
            eglast=sq["eglast"][:, ld:ld + 1])

    hd = [prep(j, b, h) for j, b, h in items]
    kk = [_dot_nt(d["lhs_kk"], d["k_b"]) for d in hd]
    qk = [(kk[i][CHUNK:] * hd[i]["decay"]).astype(BF16) for i in idx]

    b_pow = [-jnp.where(strict, kk[i][:CHUNK] * hd[i]["decay"], 0.0) for i in idx]
    x_inv = [eye + b for b in b_pow]
    b_pow = [_dot_b(b, b) for b in b_pow]
    n_sq = int(math.log2(CHUNK)) - 1
    for it in range(n_sq):
        if it < n_sq - 1:
            p = [_dot_b(jnp.concatenate([b_pow[i], x_inv[i]], axis=0), b_pow[i]) for i in idx]
            b_pow = [p[i][:CHUNK] for i in idx]
            x_inv = [x_inv[i] + p[i][CHUNK:] for i in idx]
        else:
            x_inv = [x_inv[i] + _dot_b(x_inv[i], b_pow[i]) for i in idx]

    uw = [_dot(x_inv[i].astype(BF16), hd[i]["rhs_uw"]) for i in idx]
    state = [state_ref[n] for n in range(n_chain)]
    out = [None] * len(items)
    for j in range(nsub):
        sub = range(j * n_chain, (j + 1) * n_chain)
        ws_qs = {i: _dot(jnp.concatenate([uw[i][:, dk:].astype(BF16), hd[i]["q_dec"]], axis=0),
                         state[i - j * n_chain].astype(BF16)) for i in sub}
        v_new = {i: (uw[i][:, :dk] - ws_qs[i][:CHUNK]).astype(BF16) for i in sub}
        for i in sub:
            out[i] = ws_qs[i][CHUNK:] + _dot(qk[i], v_new[i])
        state = [state[i - j * n_chain] * hd[i]["eglast"] + _dot(hd[i]["k_tail_t"], v_new[i]) for i in sub]
    for n in range(n_chain):
        state_ref[n] = state[n]
    for i, (j, b, h) in enumerate(items):
        lo = h * dk
        rows = slice(j * CHUNK, (j + 1) * CHUNK)
        o = out[i] * lax.rsqrt(jnp.mean(out[i] * out[i], axis=-1, keepdims=True) + NORM_EPS) * normw_ref[...]
        o_ref[b, rows, lo:lo + dk] = (o * _silu(z_ref[b, rows, lo:lo + dk])).astype(o_ref.dtype)


def _lane_vec(values, lane0):
    v = jnp.zeros((1, LANES), F32)
    return lax.dynamic_update_slice(v, values.astype(F32).reshape(1, -1), (0, lane0))


def _gdn(proj, small, conv_w, a_log, dt_bias, norm_w):
    bsz, seqlen, _ = proj.shape
    nc = seqlen // CHUNK
    wq = 3 * GDN_WIDTH
    nsub = GDN_CHUNKS_PER_STEP if nc % GDN_CHUNKS_PER_STEP == 0 else 1
    tb = nsub * CHUNK
    rows_per_halo = tb // HALO
    return pl.pallas_call(
        _gdn_kernel,
        out_shape=jax.ShapeDtypeStruct((bsz, seqlen, GDN_WIDTH), F32),
        grid=(nc // nsub,),
        in_specs=[
            pl.BlockSpec((bsz, tb, wq), lambda c: (0, c, OFF_QKV // wq)),
            pl.BlockSpec((bsz, HALO, wq),
                         lambda c: (0, jnp.maximum(c * rows_per_halo - 1, 0), OFF_QKV // wq)),
            pl.BlockSpec((bsz, tb, GDN_WIDTH), lambda c: (0, c, OFF_AZ // GDN_WIDTH)),
            pl.BlockSpec((bsz, tb, LANES), lambda c: (0, c, 0)),
            pl.BlockSpec((CONV_K, wq), lambda c: (0, 0)),
            pl.BlockSpec((1, LANES), lambda c: (0, 0)),
            pl.BlockSpec((1, LANES), lambda c: (0, 0)),
            pl.BlockSpec((1, GDN_HEAD_DIM), lambda c: (0, 0)),
        ],
        out_specs=pl.BlockSpec((bsz, tb, GDN_WIDTH), lambda c: (0, c, 0)),
        scratch_shapes=[pltpu.VMEM((bsz * GDN_HEADS, GDN_HEAD_DIM, GDN_HEAD_DIM), F32),
                        pltpu.VMEM((bsz, HALO + tb, wq), F32)],
        compiler_params=pltpu.CompilerParams(
            dimension_semantics=("arbitrary",), vmem_limit_bytes=VMEM_LIMIT),
        name="gdn",
    )(proj, proj, proj, small, conv_w.astype(F32), _lane_vec(a_log, LANE_DECAY),
      _lane_vec(dt_bias, LANE_DECAY), norm_w.astype(F32).reshape(1, GDN_HEAD_DIM))


def _ssd_kernel(x_ref, bc_ref, xprev_ref, bcprev_ref, z_ref, sm_ref, convw_ref, convb_ref,
                alog_ref, dtb_ref, dskip_ref, normw_ref, expand_ref,
                o_ref, state_ref, ext_ref):
    c = pl.program_id(0)
    first = c == 0
    nb = x_ref.shape[0]

    @pl.when(first)
    def _():
        state_ref[...] = jnp.zeros_like(state_ref)

    causal = _tri(CHUNK)
    ltri = causal.astype(BF16)
    expand = expand_ref[...]

    def per_seq(b):
        prev_x = xprev_ref[b]
        prev_bc = bcprev_ref[b]
        ext_ref[b, 0:HALO, 0:M2_WIDTH] = jnp.where(first, jnp.zeros_like(prev_x), prev_x)
        ext_ref[b, 0:HALO, M2_WIDTH:] = jnp.where(first, jnp.zeros_like(prev_bc), prev_bc)
        ext_ref[b, HALO:, 0:M2_WIDTH] = x_ref[b]
        ext_ref[b, HALO:, M2_WIDTH:] = bc_ref[b]
        dt_all = _softplus(sm_ref[b] + dtb_ref[...])
        a_all = -jnp.exp(alog_ref[...]) * dt_all
        acum = _cumsum_rows(ltri, a_all)
        ea = jnp.exp(acum)
        ds = jnp.exp(acum[CHUNK - 1:CHUNK, :] - acum)
        sp_hi, sp_lo = _split_bf16(jnp.concatenate([dt_all, ea, ds], axis=0), 2)
        spread = _dot(jnp.concatenate([sp_hi, sp_lo], axis=0), expand)
        spread = spread[:3 * CHUNK] + spread[3 * CHUNK:]
        ea_x = spread[CHUNK:2 * CHUNK]
        return dict(acum=acum, acum_t=acum.T, dt_x=spread[:CHUNK], ea_x=ea_x, ds_x=spread[2 * CHUNK:],
                    cdecay_x=ea_x[CHUNK - 1:CHUNK, :])

    seqs = [per_seq(b) for b in range(nb)]

    lane = lax.broadcasted_iota(jnp.int32, (CHUNK, 2 * M2_HEAD_DIM), 1)
    heads_per_group = M2_HEADS // M2_GROUPS

    items = [(b, g) for b in range(nb) for g in range(M2_GROUPS)]
    idx = range(len(items))

    def prep(b, g):
        glo = g * M2_GROUP_W
        bcol = M2_WIDTH + g * M2_STATE
        ccol = M2_WIDTH + M2_BC + g * M2_STATE
        ext_b = ext_ref.at[b]
        bm = _silu(_conv_tile(ext_b, convw_ref, bcol, M2_STATE) + convb_ref[:, bcol:bcol + M2_STATE])
        cm = _silu(_conv_tile(ext_b, convw_ref, ccol, M2_STATE) + convb_ref[:, ccol:ccol + M2_STATE])
        xs = _silu(_conv_tile(ext_b, convw_ref, glo, M2_GROUP_W) + convb_ref[:, glo:glo + M2_GROUP_W])
        xdt = xs * seqs[b]["dt_x"][:, glo:glo + M2_GROUP_W]
        return dict(xs=xs, xdt_b=xdt.astype(BF16),
                    xds=(xdt * seqs[b]["ds_x"][:, glo:glo + M2_GROUP_W]).astype(BF16),
                    bm_b=bm.astype(BF16), bm_t=bm.T.astype(BF16), cm_b=cm.astype(BF16))

    gd = [prep(b, g) for b, g in items]
    scores = [_dot_nt(d["cm_b"], d["bm_b"]) for d in gd]
    s_prev = [state_ref[i] for i in idx]
    y_off = [_dot(gd[i]["cm_b"], s_prev[i].astype(BF16)) for i in idx]
    states = [_dot(d["bm_t"], d["xds"]) for d in gd]
    for i, (b, g) in enumerate(items):
        glo = g * M2_GROUP_W
        state_ref[i] = s_prev[i] * seqs[b]["cdecay_x"][:, glo:glo + M2_GROUP_W] + states[i]

    def diag(i, j, r):
        b, g = items[i]
        la = LANE_DT + g * heads_per_group + 2 * j + r
        seg = jnp.exp(jnp.where(causal, seqs[b]["acum"][:, la:la + 1] - seqs[b]["acum_t"][la:la + 1, :],
                                -jnp.inf))
        x2 = gd[i]["xdt_b"][:, j * 2 * M2_HEAD_DIM:(j + 1) * 2 * M2_HEAD_DIM]
        return _dot((scores[i] * seg).astype(BF16), x2)

    pairs = range(heads_per_group // 2)
    yd = [[[diag(i, j, r) for r in range(2)] for j in pairs] for i in idx]
    for i, (b, g) in enumerate(items):
        glo = g * M2_GROUP_W
        parts = [jnp.where(lane < M2_HEAD_DIM, yd[i][j][0], yd[i][j][1]) for j in pairs]
        y = jnp.concatenate(parts, axis=1) + y_off[i] * seqs[b]["ea_x"][:, glo:glo + M2_GROUP_W]
        y = y + dskip_ref[:, glo:glo + M2_GROUP_W] * gd[i]["xs"]
        y = y * _silu(z_ref[b, :, glo:glo + M2_GROUP_W])
        y = y * lax.rsqrt(jnp.mean(y * y, axis=-1, keepdims=True) + NORM_EPS)
        o_ref[b, :, glo:glo + M2_GROUP_W] = (y * normw_ref[:, glo:glo + M2_GROUP_W]).astype(o_ref.dtype)


def _ssd(proj, small, conv_w, conv_b, a_log, dt_bias, d_skip, norm_w):
    bsz, seqlen, _ = proj.shape
    nc = seqlen // CHUNK
    rows_per_halo = CHUNK // HALO
    conv_dim = M2_WIDTH + 2 * M2_BC
    head_of_col = jnp.arange(M2_WIDTH) // M2_HEAD_DIM
    expand = (jnp.arange(LANES)[:, None] == (LANE_DT + head_of_col)[None, :]).astype(BF16)
    dskip_x = jnp.repeat(d_skip.astype(F32), M2_HEAD_DIM).reshape(1, M2_WIDTH)
    xi = OFF_XBC // M2_WIDTH

    def halo_idx(c):
        return jnp.maximum(c * rows_per_halo - 1, 0)

    return pl.pallas_call(
        _ssd_kernel,
        out_shape=jax.ShapeDtypeStruct((bsz, seqlen, M2_WIDTH), F32),
        grid=(nc,),
        in_specs=[
            pl.BlockSpec((bsz, CHUNK, M2_WIDTH), lambda c: (0, c, xi)),
            pl.BlockSpec((bsz, CHUNK, 2 * M2_BC), lambda c: (0, c, xi + 1)),
            pl.BlockSpec((bsz, HALO, M2_WIDTH), lambda c: (0, halo_idx(c), xi)),
            pl.BlockSpec((bsz, HALO, 2 * M2_BC), lambda c: (0, halo_idx(c), xi + 1)),
            pl.BlockSpec((bsz, CHUNK, M2_WIDTH), lambda c: (0, c, OFF_CZ // M2_WIDTH)),
            pl.BlockSpec((bsz, CHUNK, LANES), lambda c: (0, c, 0)),
            pl.BlockSpec((CONV_K, conv_dim), lambda c: (0, 0)),
            pl.BlockSpec((1, conv_dim), lambda c: (0, 0)),
            pl.BlockSpec((1, LANES), lambda c: (0, 0)),
            pl.BlockSpec((1, LANES), lambda c: (0, 0)),
            pl.BlockSpec((1, M2_WIDTH), lambda c: (0, 0)),
            pl.BlockSpec((1, M2_WIDTH), lambda c: (0, 0)),
            pl.BlockSpec((LANES, M2_WIDTH), lambda c: (0, 0)),
        ],
        out_specs=pl.BlockSpec((bsz, CHUNK, M2_WIDTH), lambda c: (0, c, 0)),
        scratch_shapes=[pltpu.VMEM((bsz * M2_GROUPS, M2_STATE, M2_GROUP_W), F32),
                        pltpu.VMEM((bsz, HALO + CHUNK, conv_dim), F32)],
        compiler_params=pltpu.CompilerParams(
            dimension_semantics=("arbitrary",), vmem_limit_bytes=VMEM_LIMIT),
        name="ssd",
    )(proj, proj, proj, proj, proj, small, conv_w.astype(F32), conv_b.astype(F32).reshape(1, conv_dim),
      _lane_vec(a_log, LANE_DT), _lane_vec(dt_bias, LANE_DT), dskip_x,
      norm_w.astype(F32).reshape(1, M2_WIDTH), expand)


def _s5_kernel(u_ref, lamre_c_ref, lamim_c_ref, lamre_r_ref, lamim_r_ref, lstep_ref,
               bre_c_ref, bim_c_ref, bre_r_ref, bim_r_ref, cre_c_ref, cim_c_ref,
               o_ref, toep_ref, *, chunks_per_seq):
    p = S5_STATE
    hs = S5_GROUP_SIZE
    cw = CHUNK * hs
    step = jnp.exp(lstep_ref[0:1, :])
    step_r = step[:, :p]

    def disc(lre, lim, st):
        lre = jnp.minimum(lre, -1e-4)
        mag = jnp.exp(lre * st)
        are = mag * jnp.cos(lim * st)
        aim = mag * jnp.sin(lim * st)
        den = lre * lre + lim * lim
        fre = ((are - 1.0) * lre + aim * lim) / den
        fim = (aim * lre - (are - 1.0) * lim) / den
        return lre, are, aim, fre, fim

    lre_c, are_c, aim_c, fre_c, fim_c = disc(lamre_c_ref[...], lamim_c_ref[...], step)
    lim_c = lamim_c_ref[...]
    lre_r, _, _, fre_r, fim_r = disc(lamre_r_ref[0:1, :], lamim_r_ref[0:1, :], step_r)
    lim_r = lamim_r_ref[0:1, :]
    bbre_r = fre_r * bre_r_ref[...] - fim_r * bim_r_ref[...]
    bbim_r = fre_r * bim_r_ref[...] + fim_r * bre_r_ref[...]

    assert LANES % CHUNK == 0
    ntile = cw // LANES
    t_lane = (lax.broadcasted_iota(jnp.int32, (p, LANES), 1) & (CHUNK - 1)).astype(F32)

    def cmul(a, b):
        return a[0] * b[0] - a[1] * b[1], a[0] * b[1] + a[1] * b[0]

    def cpow(expo):
        mag = jnp.exp(lre_c * step * expo)
        ang = lim_c * step * expo
        return mag * jnp.cos(ang), mag * jnp.sin(ang)

    def tiled(z):
        return (jnp.concatenate([z[0]] * ntile, axis=1), jnp.concatenate([z[1]] * ntile, axis=1))

    m0re, m0im = cmul((cre_c_ref[...], cim_c_ref[...]), tiled(cpow(t_lane)))
    m1re, m1im = cmul((m0re, m0im), tiled((are_c, aim_c)))
    bb = cmul(tiled((fre_c, fim_c)), (bre_c_ref[...], bim_c_ref[...]))
    nre, nim = cmul(bb, tiled(cpow(float(CHUNK - 1) - t_lane)))

    kt = _dot_hi(bbre_r, m0re) - _dot_hi(bbim_r, m0im)

    causal_lane = ((lax.broadcasted_iota(jnp.int32, (CHUNK, cw), 1) & (CHUNK - 1))
                   >= lax.broadcasted_iota(jnp.int32, (CHUNK, cw), 0))
    for hi in range(hs):
        rows_hi = jnp.broadcast_to(kt[hi:hi + 1, :], (CHUNK, cw))
        shifted = pltpu.roll(rows_hi, 0, 1, stride=1, stride_axis=0)
        toep_ref[hi * CHUNK:(hi + 1) * CHUNK, :] = jnp.where(causal_lane, shifted, 0.0).astype(BF16)

    ub = u_ref[...]
    y = _dot(ub, toep_ref[...])
    sre = _dot_nt(ub, nre.astype(BF16))
    sim = _dot_nt(ub, nim.astype(BF16))

    rows = ub.shape[0]
    cidx = lax.broadcasted_iota(jnp.int32, (rows, p), 0) & (chunks_per_seq - 1)
    mag_c = jnp.exp(lre_r * step_r * float(CHUNK))
    pre = mag_c * jnp.cos(lim_r * step_r * float(CHUNK))
    pim = mag_c * jnp.sin(lim_r * step_r * float(CHUNK))
    d = 1
    while d < chunks_per_seq:
        ok = cidx >= d
        shre = jnp.where(ok, pltpu.roll(sre, d, 0), 0.0)
        shim = jnp.where(ok, pltpu.roll(sim, d, 0), 0.0)
        sre, sim = sre + pre * shre - pim * shim, sim + pre * shim + pim * shre
        pre, pim = pre * pre - pim * pim, 2.0 * pre * pim
        d *= 2
    ok = cidx >= 1
    xsre = jnp.where(ok, pltpu.roll(sre, 1, 0), 0.0)
    xsim = jnp.where(ok, pltpu.roll(sim, 1, 0), 0.0)
    y = y + _dot(xsre.astype(BF16), m1re.astype(BF16)) - _dot(xsim.astype(BF16), m1im.astype(BF16))
    o_ref[...] = y.astype(o_ref.dtype)


def _s5(u3, lam_re, lam_im, log_step, b_re, b_im, c_re, c_im, chunks_per_seq):
    g = lam_re.shape[0]
    rows = u3.shape[0]
    cw = u3.shape[1] // g
    p, hs = S5_STATE, S5_GROUP_SIZE
    f32 = lambda a: a.astype(F32)
    col = lambda a: jnp.broadcast_to(f32(a)[:, :, None], (g, p, LANES))
    row = lambda a: jnp.broadcast_to(f32(a)[:, None, :], (g, HALO, p))
    lstep = jnp.broadcast_to(f32(log_step)[:, None, None], (g, HALO, LANES))
    b_c = lambda a: jnp.repeat(f32(a), CHUNK, axis=2)
    b_r = lambda a: jnp.swapaxes(f32(a), 1, 2)
    c_c = lambda a: jnp.repeat(jnp.swapaxes(f32(a), 1, 2), CHUNK, axis=2)
    io_spec = pl.BlockSpec((rows, cw), lambda i: (0, i))

    def gspec(shape):
        return pl.BlockSpec((None,) + shape, lambda i: (i, 0, 0))

    return pl.pallas_call(
        functools.partial(_s5_kernel, chunks_per_seq=chunks_per_seq),
        out_shape=jax.ShapeDtypeStruct((rows, g * cw), BF16),
        grid=(g,),
        in_specs=[io_spec,
                  gspec((p, LANES)), gspec((p, LANES)), gspec((HALO, p)), gspec((HALO, p)),
                  gspec((HALO, LANES)),
                  gspec((p, cw)), gspec((p, cw)), gspec((hs, p)), gspec((hs, p)),
                  gspec((p, cw)), gspec((p, cw))],
        out_specs=io_spec,
        scratch_shapes=[pltpu.VMEM((cw, cw), BF16)],
        compiler_params=pltpu.CompilerParams(
            dimension_semantics=("parallel",), vmem_limit_bytes=VMEM_LIMIT),
        name="s5",
    )(u3, col(lam_re), col(lam_im), row(lam_re), row(lam_im), lstep,
      b_c(b_re), b_c(b_im), b_r(b_re), b_r(b_im), c_c(c_re), c_c(c_im))


def _merge_kernel(x_ref, gate_ref, ya_ref, ys_ref, su0_ref, su1_ref, su2_ref, sg0_ref, sg1_ref, sg2_ref, yc_ref,
                  sd_ref, gluw_ref, glub_ref, pa_ref, pb_ref, pc_ref, wout_ref, nw_ref, o_ref, *h_ref, final):
    d = x_ref.shape[-1]
    su = jnp.concatenate([su0_ref[...], su1_ref[...], su2_ref[...]], axis=1)
    ys = jax.nn.gelu(ys_ref[...].astype(F32) + sd_ref[...] * su)
    glu = _sigmoid(_dot(ys.astype(BF16), gluw_ref[...]) + glub_ref[...])
    sgate = jnp.concatenate([sg0_ref[...], sg1_ref[...], sg2_ref[...]], axis=1)
    yb = ys * glu * _silu(sgate)
    pa = _dot(ya_ref[...].astype(BF16), pa_ref[...])
    pb = _dot(yb.astype(BF16), pb_ref[...])
    pc = _dot(yc_ref[...].astype(BF16), pc_ref[...])
    merged = (_sigmoid(gate_ref[:, 0:d]) * pa
              + _sigmoid(gate_ref[:, d:2 * d]) * pb
              + _sigmoid(gate_ref[:, 2 * d:3 * d]) * pc)
    out = x_ref[...] + _dot(merged.astype(BF16), wout_ref[...])
    normed = out * lax.rsqrt(jnp.mean(out * out, axis=-1, keepdims=True) + NORM_EPS) * nw_ref[...]
    if final:
        o_ref[...] = normed.astype(o_ref.dtype)
    else:
        o_ref[...] = out.astype(o_ref.dtype)
        h_ref[0][...] = normed.astype(h_ref[0].dtype)


def _merge(x2d, proj2d, ya, ys, yc, s5_d, glu_w, glu_b, proj_a, proj_b, proj_c, w_out, next_norm_w, final, tm=256):
    t, d = x2d.shape
    tm = min(tm, t)
    sgw = S5_WIDTH // 3
    assert OFF_SG % sgw == 0 and OFF_SU % sgw == 0 and sgw % LANES == 0

    def wspec(shape):
        return pl.BlockSpec(shape, lambda i: (0, 0), pipeline_mode=pl.Buffered(1))

    row_spec = pl.BlockSpec((tm, d), lambda i: (i, 0))
    if final:
        out_shape, out_specs = jax.ShapeDtypeStruct((t, d), F32), row_spec
    else:
        out_shape = (jax.ShapeDtypeStruct((t, d), F32), jax.ShapeDtypeStruct((t, d), BF16))
        out_specs = (row_spec, row_spec)

    return pl.pallas_call(
        functools.partial(_merge_kernel, final=final),
        out_shape=out_shape,
        grid=(t // tm,),
        in_specs=[
            pl.BlockSpec((tm, d), lambda i: (i, 0)),
            pl.BlockSpec((tm, 3 * d), lambda i: (i, OFF_MERGE // (3 * d))),
            pl.BlockSpec((tm, GDN_WIDTH), lambda i: (i, 0)),
            pl.BlockSpec((tm, S5_WIDTH), lambda i: (i, 0)),
            pl.BlockSpec((tm, sgw), lambda i: (i, OFF_SU // sgw)),
            pl.BlockSpec((tm, sgw), lambda i: (i, OFF_SU // sgw + 1)),
            pl.BlockSpec((tm, sgw), lambda i: (i, OFF_SU // sgw + 2)),
            pl.BlockSpec((tm, sgw), lambda i: (i, OFF_SG // sgw)),
            pl.BlockSpec((tm, sgw), lambda i: (i, OFF_SG // sgw + 1)),
            pl.BlockSpec((tm, sgw), lambda i: (i, OFF_SG // sgw + 2)),
            pl.BlockSpec((tm, M2_WIDTH), lambda i: (i, 0)),
            wspec((1, S5_WIDTH)),
            wspec((S5_WIDTH, S5_WIDTH)), wspec((1, S5_WIDTH)),
            wspec((GDN_WIDTH, d)), wspec((S5_WIDTH, d)), wspec((M2_WIDTH, d)), wspec((d, d)),
            wspec((1, d)),
        ],
        out_specs=out_specs,
        compiler_params=pltpu.CompilerParams(
            dimension_semantics=("parallel",), vmem_limit_bytes=VMEM_LIMIT),
        name="merge",
    )(x2d, proj2d, ya, ys, proj2d, proj2d, proj2d, proj2d, proj2d, proj2d, yc,
      s5_d.astype(F32).reshape(1, S5_WIDTH), glu_w.astype(BF16), glu_b.astype(F32).reshape(1, -1),
      proj_a.astype(BF16), proj_b.astype(BF16), proj_c.astype(BF16), w_out.astype(BF16),
      next_norm_w.astype(F32).reshape(1, d))


def _w_in_segments(d_model):
    conv_dim = M2_WIDTH + 2 * M2_BC
    o = 0
    seg = {}
    for name, width in (("qkv", 3 * GDN_WIDTH), ("az", GDN_WIDTH), ("beta", GDN_HEADS), ("decay", GDN_HEADS),
                        ("su", S5_WIDTH), ("sg", S5_WIDTH), ("cz", M2_WIDTH), ("xbc", conv_dim),
                        ("dt", M2_HEADS), ("merge", 3 * d_model)):
        seg[name] = (o, width)
        o += width
    return seg


_MAIN_ORDER = (("merge", OFF_MERGE), ("qkv", OFF_QKV), ("xbc", OFF_XBC), ("az", OFF_AZ), ("cz", OFF_CZ),
               ("su", OFF_SU), ("sg", OFF_SG))


def _inproj_kernel(src_ref, h_ref, wt_ref, o_ref):
    del src_ref
    o_ref[...] = _dot_nt(h_ref[...], wt_ref[...].astype(BF16)).astype(o_ref.dtype)


def _inproj(h, wt_all, layer):
    m, k = h.shape
    seg = _w_in_segments(k)
    tm, tn = min(INPROJ_TM, m), INPROJ_TN

    def src_of(col):
        name, dst = [(nm, d0) for nm, d0 in _MAIN_ORDER if d0 <= col < d0 + seg[nm][1]][0]
        return seg[name][0] + col - dst

    src_rows = [src_of(j * tn) for j in range(MAIN_W // tn)]
    assert all(src_of(j * tn + tn - 1) == s + tn - 1 for j, s in enumerate(src_rows))
    return pl.pallas_call(
        _inproj_kernel,
        out_shape=jax.ShapeDtypeStruct((m, MAIN_W), F32),
        grid_spec=pltpu.PrefetchScalarGridSpec(
            num_scalar_prefetch=1,
            grid=(m // tm, MAIN_W // tn),
            in_specs=[pl.BlockSpec((tm, k), lambda i, j, src: (i, 0)),
                      pl.BlockSpec((None, pl.Element(tn), pl.Element(k)),
                                   lambda i, j, src: (layer, pl.multiple_of(src[j], HALO), 0))],
            out_specs=pl.BlockSpec((tm, tn), lambda i, j, src: (i, j))),
        compiler_params=pltpu.CompilerParams(
            dimension_semantics=("parallel", "arbitrary"), vmem_limit_bytes=VMEM_LIMIT),
        name="inproj",
    )(jnp.asarray(src_rows, jnp.int32), h, wt_all)


def _pack_small_kernel(gdn_ref, dt_ref, o_ref):
    o_ref[...] = jnp.zeros_like(o_ref)
    o_ref[LANE_BETA:LANE_BETA + 2 * GDN_HEADS, :] = gdn_ref[...].astype(o_ref.dtype)
    o_ref[LANE_DT:LANE_DT + M2_HEADS, :] = dt_ref[...].astype(o_ref.dtype)


def _pack_small(wt_all, layer):
    _, n, k = wt_all.shape
    seg = _w_in_segments(k)
    n_gdn = 2 * GDN_HEADS
    assert seg["decay"][0] == seg["beta"][0] + GDN_HEADS
    return pl.pallas_call(
        _pack_small_kernel,
        out_shape=jax.ShapeDtypeStruct((LANES, k), BF16),
        grid=(1,),
        in_specs=[pl.BlockSpec((None, pl.Element(n_gdn), pl.Element(k)), lambda i: (layer, seg["beta"][0], 0)),
                  pl.BlockSpec((None, pl.Element(M2_HEADS), pl.Element(k)), lambda i: (layer, seg["dt"][0], 0))],
        out_specs=pl.BlockSpec((LANES, k), lambda i: (0, 0)),
        name="packs",
    )(wt_all, wt_all)


def _layer(x2d, h, bsz, seqlen, w_in_all, layer, gdn_conv_w, gdn_a_log, gdn_dt_bias, gdn_norm_w,
           s5_lam_re, s5_lam_im, s5_log_step, s5_b_re, s5_b_im, s5_c_re, s5_c_im, s5_d,
           s5_glu_w, s5_glu_b, m2_conv_w, m2_conv_b, m2_a_log, m2_dt_bias, m2_d, m2_norm_w,
           proj_a, proj_b, proj_c, w_out, next_norm_w, final):
    t = bsz * seqlen
    nc = seqlen // CHUNK
    proj = _inproj(h, w_in_all, layer)
    small = _matmul(h, _pack_small(w_in_all, layer), INPROJ_TM, LANES, rhs_transposed=True)
    proj3 = proj.reshape(bsz, seqlen, MAIN_W)
    small3 = small.reshape(bsz, seqlen, LANES)

    ya = _gdn(proj3, small3, gdn_conv_w, gdn_a_log, gdn_dt_bias, gdn_norm_w)
    yc = _ssd(proj3, small3, m2_conv_w, m2_conv_b, m2_a_log, m2_dt_bias, m2_d, m2_norm_w)

    u = proj[:, OFF_SU:OFF_SU + S5_WIDTH].astype(BF16).reshape(bsz * nc, CHUNK, S5_WIDTH)
    u = jnp.swapaxes(u, 1, 2).reshape(bsz * nc, S5_WIDTH * CHUNK)
    ys = _s5(u, s5_lam_re, s5_lam_im, s5_log_step, s5_b_re, s5_b_im, s5_c_re, s5_c_im, nc)
    ys = jnp.swapaxes(ys.reshape(bsz * nc, S5_WIDTH, CHUNK), 1, 2).reshape(t, S5_WIDTH)

    return _merge(x2d, proj, ya.reshape(t, GDN_WIDTH), ys, yc.reshape(t, M2_WIDTH), s5_d,
                  s5_glu_w, s5_glu_b, proj_a, proj_b, proj_c, w_out, next_norm_w, final)


def kernel(x, norm_w, w_in, gdn_conv_w, gdn_a_log, gdn_dt_bias, gdn_norm_w, s5_lam_re, s5_lam_im, s5_log_step, s5_b_re, s5_b_im, s5_c_re, s5_c_im, s5_d, s5_glu_w, s5_glu_b, m2_conv_w, m2_conv_b, m2_a_log, m2_dt_bias, m2_d, m2_norm_w, proj_a, proj_b, proj_c, w_out, final_norm_w):
    bsz, seqlen, d = x.shape
    depth = norm_w.shape[0]
    x2d = x.reshape(bsz * seqlen, d)
    h = _rmsnorm(x2d, norm_w[0], BF16)
    w_in = jnp.swapaxes(w_in, 1, 2)
    for i in range(depth):
        final = i == depth - 1
        res = _layer(x2d, h, bsz, seqlen, w_in, i, gdn_conv_w[i], gdn_a_log[i], gdn_dt_bias[i],
                     gdn_norm_w[i], s5_lam_re[i], s5_lam_im[i], s5_log_step[i], s5_b_re[i], s5_b_im[i],
                     s5_c_re[i], s5_c_im[i], s5_d[i], s5_glu_w[i], s5_glu_b[i],
                     m2_conv_w[i], m2_conv_b[i], m2_a_log[i], m2_dt_bias[i], m2_d[i], m2_norm_w[i],
                     proj_a[i], proj_b[i], proj_c[i], w_out[i],
                     final_norm_w if final else norm_w[i + 1], final)
        if final:
            return res.reshape(bsz, seqlen, d)
        x2d, h = res
```

```python
import functools
import math

import jax
import jax.numpy as jnp
from jax import lax
from jax.experimental import pallas as pl
from jax.experimental.pallas import tpu as pltpu

F32 = jnp.float32
BF16 = jnp.bfloat16
HIGHEST = lax.Precision.HIGHEST

NORM_EPS = 1e-6
CHUNK = 64
CONV_K = 4
HALO = 8
LANES = 128

GDN_HEADS = 8
GDN_HEAD_DIM = 128
GDN_WIDTH = GDN_HEADS * GDN_HEAD_DIM
S5_GROUP_SIZE = 16
S5_GROUPS = 48
S5_STATE = 64
S5_WIDTH = S5_GROUPS * S5_GROUP_SIZE
M2_HEADS = 16
M2_HEAD_DIM = 64
M2_WIDTH = M2_HEADS * M2_HEAD_DIM
M2_GROUPS = 4
M2_STATE = 128
M2_BC = M2_GROUPS * M2_STATE
M2_GROUP_W = M2_WIDTH // M2_GROUPS

OFF_MERGE = 0
OFF_QKV = OFF_MERGE + 3 * 2048
OFF_XBC = OFF_QKV + 3 * GDN_WIDTH
OFF_AZ = OFF_XBC + M2_WIDTH + 2 * M2_BC
OFF_CZ = OFF_AZ + GDN_WIDTH
OFF_SU = OFF_CZ + M2_WIDTH
OFF_SG = OFF_SU + S5_WIDTH
MAIN_W = OFF_SG + S5_WIDTH
LANE_BETA = 0
LANE_DECAY = GDN_HEADS
LANE_DT = 2 * GDN_HEADS

VMEM_LIMIT = 56 * 1024 * 1024
INPROJ_TM = 2048
INPROJ_TN = 512
GDN_CHUNKS_PER_STEP = 1


def _softplus(x):
    return jnp.maximum(x, 0.0) + jnp.log(1.0 + jnp.exp(-jnp.abs(x)))


def _sigmoid(x):
    return 0.5 * jnp.tanh(0.5 * x) + 0.5


def _silu(x):
    h = 0.5 * x
    return h + h * jnp.tanh(h)


def _dot(a, b):
    return jnp.dot(a, b, preferred_element_type=F32)


def _dot_nt(a, b):
    return lax.dot_general(a, b, (((1,), (1,)), ((), ())), preferred_element_type=F32)


def _dot_hi(a, b):
    return jnp.dot(a, b, preferred_element_type=F32, precision=HIGHEST)


def _dot_b(a, b):
    return _dot(a.astype(BF16), b.astype(BF16))


def _split_bf16(x, terms):
    parts = []
    for _ in range(terms):
        p = x.astype(BF16)
        parts.append(p)
        x = x - p.astype(F32)
    return parts


def _cumsum_rows(ltri_b, x):
    n = x.shape[1]
    s = _dot(ltri_b, jnp.concatenate(_split_bf16(x, 3), axis=1))
    return s[:, :n] + s[:, n:2 * n] + s[:, 2 * n:]


def _tri(n, strict=False):
    r = lax.broadcasted_iota(jnp.int32, (n, n), 0)
    c = lax.broadcasted_iota(jnp.int32, (n, n), 1)
    return (r > c) if strict else (r >= c)


def _rmsnorm_kernel(x_ref, w_ref, o_ref):
    x = x_ref[...]
    y = x * lax.rsqrt(jnp.mean(x * x, axis=-1, keepdims=True) + NORM_EPS)
    o_ref[...] = (y * w_ref[...]).astype(o_ref.dtype)


def _rmsnorm(x2d, w, out_dtype, tm=512):
    t, d = x2d.shape
    tm = min(tm, t)
    return pl.pallas_call(
        _rmsnorm_kernel,
        out_shape=jax.ShapeDtypeStruct((t, d), out_dtype),
        grid=(t // tm,),
        in_specs=[pl.BlockSpec((tm, d), lambda i: (i, 0)),
                  pl.BlockSpec((1, d), lambda i: (0, 0))],
        out_specs=pl.BlockSpec((tm, d), lambda i: (i, 0)),
        compiler_params=pltpu.CompilerParams(dimension_semantics=("parallel",)),
        name="rmsnorm",
    )(x2d, w.reshape(1, d))


def _matmul_kernel(a_ref, b_ref, o_ref, *, rhs_transposed):
    dot = _dot_nt if rhs_transposed else _dot
    o_ref[...] = dot(a_ref[...], b_ref[...]).astype(o_ref.dtype)


def _matmul(a, b, tm, tn, out_dtype=F32, rhs_transposed=False):
    m, k = a.shape
    n = b.shape[0] if rhs_transposed else b.shape[1]
    tm = min(tm, m)
    tn = min(tn, n)
    b_spec = (pl.BlockSpec((tn, k), lambda i, j: (j, 0)) if rhs_transposed
              else pl.BlockSpec((k, tn), lambda i, j: (0, j)))
    return pl.pallas_call(
        functools.partial(_matmul_kernel, rhs_transposed=rhs_transposed),
        out_shape=jax.ShapeDtypeStruct((m, n), out_dtype),
        grid=(m // tm, n // tn),
        in_specs=[pl.BlockSpec((tm, k), lambda i, j: (i, 0)), b_spec],
        out_specs=pl.BlockSpec((tm, tn), lambda i, j: (i, j)),
        compiler_params=pltpu.CompilerParams(
            dimension_semantics=("parallel", "parallel"), vmem_limit_bytes=VMEM_LIMIT),
        name="inproj",
    )(a, b)


def _fill_ext(ext_ref, prev_ref, cur_ref, first):
    prev = prev_ref[...]
    ext_ref[0:HALO, :] = jnp.where(first, jnp.zeros_like(prev), prev)
    ext_ref[HALO:, :] = cur_ref[...]


def _conv_tile(ext_ref, w_ref, col, width, row0=0):
    acc = None
    for k in range(CONV_K):
        start = row0 + HALO - (CONV_K - 1) + k
        term = w_ref[k:k + 1, col:col + width] * ext_ref[start:start + CHUNK, col:col + width]
        acc = term if acc is None else acc + term
    return acc


def _gdn_kernel(q_ref, qprev_ref, z_ref, sm_ref, convw_ref, alog_ref, dtb_ref, normw_ref,
                o_ref, state_ref, ext_ref):
    c = pl.program_id(0)
    first = c == 0
    nb = q_ref.shape[0]
    nsub = q_ref.shape[1] // CHUNK
    dk = GDN_HEAD_DIM

    @pl.when(first)
    def _():
        state_ref[...] = jnp.zeros_like(state_ref)

    causal = _tri(CHUNK)
    strict = _tri(CHUNK, strict=True)
    ltri = causal.astype(BF16)
    eye = (lax.broadcasted_iota(jnp.int32, (CHUNK, CHUNK), 0)
           == lax.broadcasted_iota(jnp.int32, (CHUNK, CHUNK), 1)).astype(F32)
    scale = dk ** -0.5

    for b in range(nb):
        _fill_ext(ext_ref.at[b], qprev_ref.at[b], q_ref.at[b], first)

    def per_seq(b, j):
        sm = sm_ref[b, j * CHUNK:(j + 1) * CHUNK, :]
        g_all = -jnp.exp(alog_ref[...]) * _softplus(sm + dtb_ref[...])
        gc_all = _cumsum_rows(ltri, g_all)
        glast_all = gc_all[CHUNK - 1:CHUNK, :]
        return dict(beta=_sigmoid(sm), gc=gc_all, gc_t=gc_all.T, eg=jnp.exp(gc_all),
                    etail=jnp.exp(glast_all - gc_all), eglast=jnp.exp(glast_all))

    seqs = {(b, j): per_seq(b, j) for b in range(nb) for j in range(nsub)}

    items = [(j, b, h) for j in range(nsub) for b in range(nb) for h in range(GDN_HEADS)]
    idx = range(len(items))
    n_chain = nb * GDN_HEADS

    def prep(j, b, h):
        lo = h * dk
        sq = seqs[b, j]
        ext_b = ext_ref.at[b]
        q = _silu(_conv_tile(ext_b, convw_ref, lo, dk, j * CHUNK))
        k = _silu(_conv_tile(ext_b, convw_ref, GDN_WIDTH + lo, dk, j * CHUNK))
        v = _silu(_conv_tile(ext_b, convw_ref, 2 * GDN_WIDTH + lo, dk, j * CHUNK))
        q = q * lax.rsqrt(jnp.sum(q * q, axis=-1, keepdims=True) + NORM_EPS) * scale
        k = k * lax.rsqrt(jnp.sum(k * k, axis=-1, keepdims=True) + NORM_EPS)
        beta = sq["beta"][:, LANE_BETA + h:LANE_BETA + h + 1]
        ld = LANE_DECAY + h
        eg = sq["eg"][:, ld:ld + 1]
        decay = jnp.exp(jnp.where(causal, sq["gc"][:, ld:ld + 1] - sq["gc_t"][ld:ld + 1, :], -jnp.inf))
        kb = k * beta
        return dict(
            lhs_kk=jnp.concatenate([kb, q], axis=0).astype(BF16), k_b=k.astype(BF16), decay=decay,
            rhs_uw=jnp.concatenate([v * beta, kb * eg], axis=1).astype(BF16),
            q_dec=(q * eg).astype(BF16),
            k_tail_t=(k * sq["etail"][:, ld:ld + 1]).T.astype(BF16),
            eglast=sq["eglast"][:, ld:ld + 1])

    hd = [prep(j, b, h) for j, b, h in items]
    kk = [_dot_nt(d["lhs_kk"], d["k_b"]) for d in hd]
    qk = [(kk[i][CHUNK:] * hd[i]["decay"]).astype(BF16) for i in idx]

    b_pow = [-jnp.where(strict, kk[i][:CHUNK] * hd[i]["decay"], 0.0) for i in idx]
    x_inv = [eye + b for b in b_pow]
    b_pow = [_dot_b(b, b) for b in b_pow]
    n_sq = int(math.log2(CHUNK)) - 1
    for it in range(n_sq):
        if it < n_sq - 1:
            p = [_dot_b(jnp.concatenate([b_pow[i], x_inv[i]], axis=0), b_pow[i]) for i in idx]
            b_pow = [p[i][:CHUNK] for i in idx]
            x_inv = [x_inv[i] + p[i][CHUNK:] for i in idx]
        else:
            x_inv = [x_inv[i] + _dot_b(x_inv[i], b_pow[i]) for i in idx]

    uw = [_dot(x_inv[i].astype(BF16), hd[i]["rhs_uw"]) for i in idx]
    state = [state_ref[n] for n in range(n_chain)]
    out = [None] * len(items)
    for j in range(nsub):
        sub = range(j * n_chain, (j + 1) * n_chain)
        ws_qs = {i: _dot(jnp.concatenate([uw[i][:, dk:].astype(BF16), hd[i]["q_dec"]], axis=0),
                         state[i - j * n_chain].astype(BF16)) for i in sub}
        v_new = {i: (uw[i][:, :dk] - ws_qs[i][:CHUNK]).astype(BF16) for i in sub}
        for i in sub:
            out[i] = ws_qs[i][CHUNK:] + _dot(qk[i], v_new[i])
        state = [state[i - j * n_chain] * hd[i]["eglast"] + _dot(hd[i]["k_tail_t"], v_new[i]) for i in sub]
    for n in range(n_chain):
        state_ref[n] = state[n]
    for i, (j, b, h) in enumerate(items):
        lo = h * dk
        rows = slice(j * CHUNK, (j + 1) * CHUNK)
        o = out[i] * lax.rsqrt(jnp.mean(out[i] * out[i], axis=-1, keepdims=True) + NORM_EPS) * normw_ref[...]
        o_ref[b, rows, lo:lo + dk] = (o * _silu(z_ref[b, rows, lo:lo + dk])).astype(o_ref.dtype)


def _lane_vec(values, lane0):
    v = jnp.zeros((1, LANES), F32)
    return lax.dynamic_update_slice(v, values.astype(F32).reshape(1, -1), (0, lane0))


def _gdn(proj, small, conv_w, a_log, dt_bias, norm_w):
    bsz, seqlen, _ = proj.shape
    nc = seqlen // CHUNK
    wq = 3 * GDN_WIDTH
    nsub = GDN_CHUNKS_PER_STEP if nc % GDN_CHUNKS_PER_STEP == 0 else 1
    tb = nsub * CHUNK
    rows_per_halo = tb // HALO
    return pl.pallas_call(
        _gdn_kernel,
        out_shape=jax.ShapeDtypeStruct((bsz, seqlen, GDN_WIDTH), F32),
        grid=(nc // nsub,),
        in_specs=[
            pl.BlockSpec((bsz, tb, wq), lambda c: (0, c, OFF_QKV // wq)),
            pl.BlockSpec((bsz, HALO, wq),
                         lambda c: (0, jnp.maximum(c * rows_per_halo - 1, 0), OFF_QKV // wq)),
            pl.BlockSpec((bsz, tb, GDN_WIDTH), lambda c: (0, c, OFF_AZ // GDN_WIDTH)),
            pl.BlockSpec((bsz, tb, LANES), lambda c: (0, c, 0)),
            pl.BlockSpec((CONV_K, wq), lambda c: (0, 0)),
            pl.BlockSpec((1, LANES), lambda c: (0, 0)),
            pl.BlockSpec((1, LANES), lambda c: (0, 0)),
            pl.BlockSpec((1, GDN_HEAD_DIM), lambda c: (0, 0)),
        ],
        out_specs=pl.BlockSpec((bsz, tb, GDN_WIDTH), lambda c: (0, c, 0)),
        scratch_shapes=[pltpu.VMEM((bsz * GDN_HEADS, GDN_HEAD_DIM, GDN_HEAD_DIM), F32),
                        pltpu.VMEM((bsz, HALO + tb, wq), F32)],
        compiler_params=pltpu.CompilerParams(
            dimension_semantics=("arbitrary",), vmem_limit_bytes=VMEM_LIMIT),
        name="gdn",
    )(proj, proj, proj, small, conv_w.astype(F32), _lane_vec(a_log, LANE_DECAY),
      _lane_vec(dt_bias, LANE_DECAY), norm_w.astype(F32).reshape(1, GDN_HEAD_DIM))


def _ssd_kernel(x_ref, bc_ref, xprev_ref, bcprev_ref, z_ref, sm_ref, convw_ref, convb_ref,
                alog_ref, dtb_ref, dskip_ref, normw_ref, expand_ref,
                o_ref, state_ref, ext_ref):
    c = pl.program_id(0)
    first = c == 0
    nb = x_ref.shape[0]

    @pl.when(first)
    def _():
        state_ref[...] = jnp.zeros_like(state_ref)

    causal = _tri(CHUNK)
    ltri = causal.astype(BF16)
    expand = expand_ref[...]

    def per_seq(b):
        prev_x = xprev_ref[b]
        prev_bc = bcprev_ref[b]
        ext_ref[b, 0:HALO, 0:M2_WIDTH] = jnp.where(first, jnp.zeros_like(prev_x), prev_x)
        ext_ref[b, 0:HALO, M2_WIDTH:] = jnp.where(first, jnp.zeros_like(prev_bc), prev_bc)
        ext_ref[b, HALO:, 0:M2_WIDTH] = x_ref[b]
        ext_ref[b, HALO:, M2_WIDTH:] = bc_ref[b]
        dt_all = _softplus(sm_ref[b] + dtb_ref[...])
        a_all = -jnp.exp(alog_ref[...]) * dt_all
        acum = _cumsum_rows(ltri, a_all)
        ea = jnp.exp(acum)
        ds = jnp.exp(acum[CHUNK - 1:CHUNK, :] - acum)
        sp_hi, sp_lo = _split_bf16(jnp.concatenate([dt_all, ea, ds], axis=0), 2)
        spread = _dot(jnp.concatenate([sp_hi, sp_lo], axis=0), expand)
        spread = spread[:3 * CHUNK] + spread[3 * CHUNK:]
        ea_x = spread[CHUNK:2 * CHUNK]
        return dict(acum=acum, acum_t=acum.T, dt_x=spread[:CHUNK], ea_x=ea_x, ds_x=spread[2 * CHUNK:],
                    cdecay_x=ea_x[CHUNK - 1:CHUNK, :])

    seqs = [per_seq(b) for b in range(nb)]

    lane = lax.broadcasted_iota(jnp.int32, (CHUNK, 2 * M2_HEAD_DIM), 1)
    heads_per_group = M2_HEADS // M2_GROUPS

    items = [(b, g) for b in range(nb) for g in range(M2_GROUPS)]
    idx = range(len(items))

    def prep(b, g):
        glo = g * M2_GROUP_W
        bcol = M2_WIDTH + g * M2_STATE
        ccol = M2_WIDTH + M2_BC + g * M2_STATE
        ext_b = ext_ref.at[b]
        bm = _silu(_conv_tile(ext_b, convw_ref, bcol, M2_STATE) + convb_ref[:, bcol:bcol + M2_STATE])
        cm = _silu(_conv_tile(ext_b, convw_ref, ccol, M2_STATE) + convb_ref[:, ccol:ccol + M2_STATE])
        xs = _silu(_conv_tile(ext_b, convw_ref, glo, M2_GROUP_W) + convb_ref[:, glo:glo + M2_GROUP_W])
        xdt = xs * seqs[b]["dt_x"][:, glo:glo + M2_GROUP_W]
        return dict(xs=xs, xdt_b=xdt.astype(BF16),
                    xds=(xdt * seqs[b]["ds_x"][:, glo:glo + M2_GROUP_W]).astype(BF16),
                    bm_b=bm.astype(BF16), bm_t=bm.T.astype(BF16), cm_b=cm.astype(BF16))

    gd = [prep(b, g) for b, g in items]
    scores = [_dot_nt(d["cm_b"], d["bm_b"]) for d in gd]
    s_prev = [state_ref[i] for i in idx]
    y_off = [_dot(gd[i]["cm_b"], s_prev[i].astype(BF16)) for i in idx]
    states = [_dot(d["bm_t"], d["xds"]) for d in gd]
    for i, (b, g) in enumerate(items):
        glo = g * M2_GROUP_W
        state_ref[i] = s_prev[i] * seqs[b]["cdecay_x"][:, glo:glo + M2_GROUP_W] + states[i]

    def diag(i, j, r):
        b, g = items[i]
        la = LANE_DT + g * heads_per_group + 2 * j + r
        seg = jnp.exp(jnp.where(causal, seqs[b]["acum"][:, la:la + 1] - seqs[b]["acum_t"][la:la + 1, :],
                                -jnp.inf))
        x2 = gd[i]["xdt_b"][:, j * 2 * M2_HEAD_DIM:(j + 1) * 2 * M2_HEAD_DIM]
        return _dot((scores[i] * seg).astype(BF16), x2)

    pairs = range(heads_per_group // 2)
    yd = [[[diag(i, j, r) for r in range(2)] for j in pairs] for i in idx]
    for i, (b, g) in enumerate(items):
        glo = g * M2_GROUP_W
        parts = [jnp.where(lane < M2_HEAD_DIM, yd[i][j][0], yd[i][j][1]) for j in pairs]
        y = jnp.concatenate(parts, axis=1) + y_off[i] * seqs[b]["ea_x"][:, glo:glo + M2_GROUP_W]
        y = y + dskip_ref[:, glo:glo + M2_GROUP_W] * gd[i]["xs"]
        y = y * _silu(z_ref[b, :, glo:glo + M2_GROUP_W])
        y = y * lax.rsqrt(jnp.mean(y * y, axis=-1, keepdims=True) + NORM_EPS)
        o_ref[b, :, glo:glo + M2_GROUP_W] = (y * normw_ref[:, glo:glo + M2_GROUP_W]).astype(o_ref.dtype)


def _ssd(proj, small, conv_w, conv_b, a_log, dt_bias, d_skip, norm_w):
    bsz, seqlen, _ = proj.shape
    nc = seqlen // CHUNK
    rows_per_halo = CHUNK // HALO
    conv_dim = M2_WIDTH + 2 * M2_BC
    head_of_col = jnp.arange(M2_WIDTH) // M2_HEAD_DIM
    expand = (jnp.arange(LANES)[:, None] == (LANE_DT + head_of_col)[None, :]).astype(BF16)
    dskip_x = jnp.repeat(d_skip.astype(F32), M2_HEAD_DIM).reshape(1, M2_WIDTH)
    xi = OFF_XBC // M2_WIDTH

    def halo_idx(c):
        return jnp.maximum(c * rows_per_halo - 1, 0)

    return pl.pallas_call(
        _ssd_kernel,
        out_shape=jax.ShapeDtypeStruct((bsz, seqlen, M2_WIDTH), F32),
        grid=(nc,),
        in_specs=[
            pl.BlockSpec((bsz, CHUNK, M2_WIDTH), lambda c: (0, c, xi)),
            pl.BlockSpec((bsz, CHUNK, 2 * M2_BC), lambda c: (0, c, xi + 1)),
            pl.BlockSpec((bsz, HALO, M2_WIDTH), lambda c: (0, halo_idx(c), xi)),
            pl.BlockSpec((bsz, HALO, 2 * M2_BC), lambda c: (0, halo_idx(c), xi + 1)),
            pl.BlockSpec((bsz, CHUNK, M2_WIDTH), lambda c: (0, c, OFF_CZ // M2_WIDTH)),
            pl.BlockSpec((bsz, CHUNK, LANES), lambda c: (0, c, 0)),
            pl.BlockSpec((CONV_K, conv_dim), lambda c: (0, 0)),
            pl.BlockSpec((1, conv_dim), lambda c: (0, 0)),
            pl.BlockSpec((1, LANES), lambda c: (0, 0)),
            pl.BlockSpec((1, LANES), lambda c: (0, 0)),
            pl.BlockSpec((1, M2_WIDTH), lambda c: (0, 0)),
            pl.BlockSpec((1, M2_WIDTH), lambda c: (0, 0)),
            pl.BlockSpec((LANES, M2_WIDTH), lambda c: (0, 0)),
        ],
        out_specs=pl.BlockSpec((bsz, CHUNK, M2_WIDTH), lambda c: (0, c, 0)),
        scratch_shapes=[pltpu.VMEM((bsz * M2_GROUPS, M2_STATE, M2_GROUP_W), F32),
                        pltpu.VMEM((bsz, HALO + CHUNK, conv_dim), F32)],
        compiler_params=pltpu.CompilerParams(
            dimension_semantics=("arbitrary",), vmem_limit_bytes=VMEM_LIMIT),
        name="ssd",
    )(proj, proj, proj, proj, proj, small, conv_w.astype(F32), conv_b.astype(F32).reshape(1, conv_dim),
      _lane_vec(a_log, LANE_DT), _lane_vec(dt_bias, LANE_DT), dskip_x,
      norm_w.astype(F32).reshape(1, M2_WIDTH), expand)


def _s5_kernel(u_ref, lamre_c_ref, lamim_c_ref, lamre_r_ref, lamim_r_ref, lstep_ref,
               bre_c_ref, bim_c_ref, bre_r_ref, bim_r_ref, cre_c_ref, cim_c_ref, rexp_ref,
               o_ref, toep_ref, *, chunks_per_seq):
    p = S5_STATE
    hs = S5_GROUP_SIZE
    cw = CHUNK * hs
    step = jnp.exp(lstep_ref[0:1, :])
    step_r = step[:, :p]

    def disc(lre, lim, st):
        lre = jnp.minimum(lre, -1e-4)
        mag = jnp.exp(lre * st)
        are = mag * jnp.cos(lim * st)
        aim = mag * jnp.sin(lim * st)
        den = lre * lre + lim * lim
        fre = ((are - 1.0) * lre + aim * lim) / den
        fim = (aim * lre - (are - 1.0) * lim) / den
        return lre, are, aim, fre, fim

    lre_c, are_c, aim_c, fre_c, fim_c = disc(lamre_c_ref[...], lamim_c_ref[...], step)
    lim_c = lamim_c_ref[...]
    lre_r, _, _, fre_r, fim_r = disc(lamre_r_ref[0:1, :], lamim_r_ref[0:1, :], step_r)
    lim_r = lamim_r_ref[0:1, :]
    bbre_r = fre_r * bre_r_ref[...] - fim_r * bim_r_ref[...]
    bbim_r = fre_r * bim_r_ref[...] + fim_r * bre_r_ref[...]

    assert LANES % CHUNK == 0
    ntile = cw // LANES
    t_lane = (lax.broadcasted_iota(jnp.int32, (p, LANES), 1) & (CHUNK - 1)).astype(F32)

    def cmul(a, b):
        return a[0] * b[0] - a[1] * b[1], a[0] * b[1] + a[1] * b[0]

    def cpow(expo):
        mag = jnp.exp(lre_c * step * expo)
        ang = lim_c * step * expo
        return mag * jnp.cos(ang), mag * jnp.sin(ang)

    def tiled(z):
        return (jnp.concatenate([z[0]] * ntile, axis=1), jnp.concatenate([z[1]] * ntile, axis=1))

    stack = jnp.concatenate([bre_c_ref[...], bim_c_ref[...], cre_c_ref[...], cim_c_ref[...]], axis=0)
    spread = _dot(jnp.concatenate(_split_bf16(stack, 3), axis=0), rexp_ref[...])
    spread = spread[:4 * p] + spread[4 * p:8 * p] + spread[8 * p:]
    bexp = (spread[:p], spread[p:2 * p])
    cexp = (spread[2 * p:3 * p], spread[3 * p:])

    m0re, m0im = cmul(cexp, tiled(cpow(t_lane)))
    m1re, m1im = cmul((m0re, m0im), tiled((are_c, aim_c)))
    bb = cmul(tiled((fre_c, fim_c)), bexp)
    nre, nim = cmul(bb, tiled(cpow(float(CHUNK - 1) - t_lane)))

    kt = _dot_hi(bbre_r, m0re) - _dot_hi(bbim_r, m0im)

    causal_lane = ((lax.broadcasted_iota(jnp.int32, (CHUNK, cw), 1) & (CHUNK - 1))
                   >= lax.broadcasted_iota(jnp.int32, (CHUNK, cw), 0))
    for hi in range(hs):
        rows_hi = jnp.broadcast_to(kt[hi:hi + 1, :], (CHUNK, cw))
        shifted = pltpu.roll(rows_hi, 0, 1, stride=1, stride_axis=0)
        toep_ref[hi * CHUNK:(hi + 1) * CHUNK, :] = jnp.where(causal_lane, shifted, 0.0).astype(BF16)

    ub = u_ref[...]
    y = _dot(ub, toep_ref[...])
    sre = _dot_nt(ub, nre.astype(BF16))
    sim = _dot_nt(ub, nim.astype(BF16))

    rows = ub.shape[0]
    cidx = lax.broadcasted_iota(jnp.int32, (rows, p), 0) & (chunks_per_seq - 1)
    mag_c = jnp.exp(lre_r * step_r * float(CHUNK))
    pre = mag_c * jnp.cos(lim_r * step_r * float(CHUNK))
    pim = mag_c * jnp.sin(lim_r * step_r * float(CHUNK))
    d = 1
    while d < chunks_per_seq:
        ok = cidx >= d
        shre = jnp.where(ok, pltpu.roll(sre, d, 0), 0.0)
        shim = jnp.where(ok, pltpu.roll(sim, d, 0), 0.0)
        sre, sim = sre + pre * shre - pim * shim, sim + pre * shim + pim * shre
        pre, pim = pre * pre - pim * pim, 2.0 * pre * pim
        d *= 2
    ok = cidx >= 1
    xsre = jnp.where(ok, pltpu.roll(sre, 1, 0), 0.0)
    xsim = jnp.where(ok, pltpu.roll(sim, 1, 0), 0.0)
    y = y + _dot(xsre.astype(BF16), m1re.astype(BF16)) - _dot(xsim.astype(BF16), m1im.astype(BF16))
    o_ref[...] = y.astype(o_ref.dtype)


def _s5(u3, lam_re, lam_im, log_step, b_re, b_im, c_re, c_im, chunks_per_seq):
    g = lam_re.shape[0]
    rows = u3.shape[0]
    cw = u3.shape[1] // g
    p, hs = S5_STATE, S5_GROUP_SIZE
    f32 = lambda a: a.astype(F32)
    col = lambda a: jnp.broadcast_to(f32(a)[:, :, None], (g, p, LANES))
    row = lambda a: jnp.broadcast_to(f32(a)[:, None, :], (g, HALO, p))
    lstep = jnp.broadcast_to(f32(log_step)[:, None, None], (g, HALO, LANES))
    b_c = f32
    b_r = lambda a: jnp.swapaxes(f32(a), 1, 2)
    c_c = lambda a: jnp.swapaxes(f32(a), 1, 2)
    rexp = (jnp.arange(hs)[:, None] == (jnp.arange(cw) // CHUNK)[None, :]).astype(BF16)
    io_spec = pl.BlockSpec((rows, cw), lambda i: (0, i))

    def gspec(shape):
        return pl.BlockSpec((None,) + shape, lambda i: (i, 0, 0))

    return pl.pallas_call(
        functools.partial(_s5_kernel, chunks_per_seq=chunks_per_seq),
        out_shape=jax.ShapeDtypeStruct((rows, g * cw), BF16),
        grid=(g,),
        in_specs=[io_spec,
                  gspec((p, LANES)), gspec((p, LANES)), gspec((HALO, p)), gspec((HALO, p)),
                  gspec((HALO, LANES)),
                  gspec((p, hs)), gspec((p, hs)), gspec((hs, p)), gspec((hs, p)),
                  gspec((p, hs)), gspec((p, hs)),
                  pl.BlockSpec((hs, cw), lambda i: (0, 0))],
        out_specs=io_spec,
        scratch_shapes=[pltpu.VMEM((cw, cw), BF16)],
        compiler_params=pltpu.CompilerParams(
            dimension_semantics=("parallel",), vmem_limit_bytes=VMEM_LIMIT),
        name="s5",
    )(u3, col(lam_re), col(lam_im), row(lam_re), row(lam_im), lstep,
      b_c(b_re), b_c(b_im), b_r(b_re), b_r(b_im), c_c(c_re), c_c(c_im), rexp)


def _merge_kernel(x_ref, gate_ref, ya_ref, ys_ref, su0_ref, su1_ref, su2_ref, sg0_ref, sg1_ref, sg2_ref, yc_ref,
                  sd_ref, gluw_ref, glub_ref, pa_ref, pb_ref, pc_ref, wout_ref, nw_ref, o_ref, *h_ref, final):
    d = x_ref.shape[-1]
    su = jnp.concatenate([su0_ref[...], su1_ref[...], su2_ref[...]], axis=1)
    ys = jax.nn.gelu(ys_ref[...].astype(F32) + sd_ref[...] * su)
    glu = _sigmoid(_dot(ys.astype(BF16), gluw_ref[...]) + glub_ref[...])
    sgate = jnp.concatenate([sg0_ref[...], sg1_ref[...], sg2_ref[...]], axis=1)
    yb = ys * glu * _silu(sgate)
    pa = _dot(ya_ref[...].astype(BF16), pa_ref[...])
    pb = _dot(yb.astype(BF16), pb_ref[...])
    pc = _dot(yc_ref[...].astype(BF16), pc_ref[...])
    merged = (_sigmoid(gate_ref[:, 0:d]) * pa
              + _sigmoid(gate_ref[:, d:2 * d]) * pb
              + _sigmoid(gate_ref[:, 2 * d:3 * d]) * pc)
    out = x_ref[...] + _dot(merged.astype(BF16), wout_ref[...])
    normed = out * lax.rsqrt(jnp.mean(out * out, axis=-1, keepdims=True) + NORM_EPS) * nw_ref[...]
    if final:
        o_ref[...] = normed.astype(o_ref.dtype)
    else:
        o_ref[...] = out.astype(o_ref.dtype)
        h_ref[0][...] = normed.astype(h_ref[0].dtype)


def _merge(x2d, proj2d, ya, ys, yc, s5_d, glu_w, glu_b, proj_a, proj_b, proj_c, w_out, next_norm_w, final, tm=256):
    t, d = x2d.shape
    tm = min(tm, t)
    sgw = S5_WIDTH // 3
    assert OFF_SG % sgw == 0 and OFF_SU % sgw == 0 and sgw % LANES == 0

    def wspec(shape):
        return pl.BlockSpec(shape, lambda i: (0, 0), pipeline_mode=pl.Buffered(1))

    row_spec = pl.BlockSpec((tm, d), lambda i: (i, 0))
    if final:
        out_shape, out_specs = jax.ShapeDtypeStruct((t, d), F32), row_spec
    else:
        out_shape = (jax.ShapeDtypeStruct((t, d), F32), jax.ShapeDtypeStruct((t, d), BF16))
        out_specs = (row_spec, row_spec)

    return pl.pallas_call(
        functools.partial(_merge_kernel, final=final),
        out_shape=out_shape,
        grid=(t // tm,),
        in_specs=[
            pl.BlockSpec((tm, d), lambda i: (i, 0)),
            pl.BlockSpec((tm, 3 * d), lambda i: (i, OFF_MERGE // (3 * d))),
            pl.BlockSpec((tm, GDN_WIDTH), lambda i: (i, 0)),
            pl.BlockSpec((tm, S5_WIDTH), lambda i: (i, 0)),
            pl.BlockSpec((tm, sgw), lambda i: (i, OFF_SU // sgw)),
            pl.BlockSpec((tm, sgw), lambda i: (i, OFF_SU // sgw + 1)),
            pl.BlockSpec((tm, sgw), lambda i: (i, OFF_SU // sgw + 2)),
            pl.BlockSpec((tm, sgw), lambda i: (i, OFF_SG // sgw)),
            pl.BlockSpec((tm, sgw), lambda i: (i, OFF_SG // sgw + 1)),
            pl.BlockSpec((tm, sgw), lambda i: (i, OFF_SG // sgw + 2)),
            pl.BlockSpec((tm, M2_WIDTH), lambda i: (i, 0)),
            wspec((1, S5_WIDTH)),
            wspec((S5_WIDTH, S5_WIDTH)), wspec((1, S5_WIDTH)),
            wspec((GDN_WIDTH, d)), wspec((S5_WIDTH, d)), wspec((M2_WIDTH, d)), wspec((d, d)),
            wspec((1, d)),
        ],
        out_specs=out_specs,
        compiler_params=pltpu.CompilerParams(
            dimension_semantics=("parallel",), vmem_limit_bytes=VMEM_LIMIT),
        name="merge",
    )(x2d, proj2d, ya, ys, proj2d, proj2d, proj2d, proj2d, proj2d, proj2d, yc,
      s5_d.astype(F32).reshape(1, S5_WIDTH), glu_w.astype(BF16), glu_b.astype(F32).reshape(1, -1),
      proj_a.astype(BF16), proj_b.astype(BF16), proj_c.astype(BF16), w_out.astype(BF16),
      next_norm_w.astype(F32).reshape(1, d))


def _w_in_segments(d_model):
    conv_dim = M2_WIDTH + 2 * M2_BC
    o = 0
    seg = {}
    for name, width in (("qkv", 3 * GDN_WIDTH), ("az", GDN_WIDTH), ("beta", GDN_HEADS), ("decay", GDN_HEADS),
                        ("su", S5_WIDTH), ("sg", S5_WIDTH), ("cz", M2_WIDTH), ("xbc", conv_dim),
                        ("dt", M2_HEADS), ("merge", 3 * d_model)):
        seg[name] = (o, width)
        o += width
    return seg


_MAIN_ORDER = (("merge", OFF_MERGE), ("qkv", OFF_QKV), ("xbc", OFF_XBC), ("az", OFF_AZ), ("cz", OFF_CZ),
               ("su", OFF_SU), ("sg", OFF_SG))


def _inproj_kernel(src_ref, h_ref, wt_ref, o_ref):
    del src_ref
    o_ref[...] = _dot_nt(h_ref[...], wt_ref[...].astype(BF16)).astype(o_ref.dtype)


def _inproj(h, wt_all, layer):
    m, k = h.shape
    seg = _w_in_segments(k)
    tm, tn = min(INPROJ_TM, m), INPROJ_TN

    def src_of(col):
        name, dst = [(nm, d0) for nm, d0 in _MAIN_ORDER if d0 <= col < d0 + seg[nm][1]][0]
        return seg[name][0] + col - dst

    src_rows = [src_of(j * tn) for j in range(MAIN_W // tn)]
    assert all(src_of(j * tn + tn - 1) == s + tn - 1 for j, s in enumerate(src_rows))
    return pl.pallas_call(
        _inproj_kernel,
        out_shape=jax.ShapeDtypeStruct((m, MAIN_W), F32),
        grid_spec=pltpu.PrefetchScalarGridSpec(
            num_scalar_prefetch=1,
            grid=(m // tm, MAIN_W // tn),
            in_specs=[pl.BlockSpec((tm, k), lambda i, j, src: (i, 0)),
                      pl.BlockSpec((None, pl.Element(tn), pl.Element(k)),
                                   lambda i, j, src: (layer, pl.multiple_of(src[j], HALO), 0))],
            out_specs=pl.BlockSpec((tm, tn), lambda i, j, src: (i, j))),
        compiler_params=pltpu.CompilerParams(
            dimension_semantics=("parallel", "arbitrary"), vmem_limit_bytes=VMEM_LIMIT),
        name="inproj",
    )(jnp.asarray(src_rows, jnp.int32), h, wt_all)


def _pack_small_kernel(gdn_ref, dt_ref, o_ref):
    o_ref[...] = jnp.zeros_like(o_ref)
    o_ref[LANE_BETA:LANE_BETA + 2 * GDN_HEADS, :] = gdn_ref[...].astype(o_ref.dtype)
    o_ref[LANE_DT:LANE_DT + M2_HEADS, :] = dt_ref[...].astype(o_ref.dtype)


def _pack_small(wt_all, layer):
    _, n, k = wt_all.shape
    seg = _w_in_segments(k)
    n_gdn = 2 * GDN_HEADS
    assert seg["decay"][0] == seg["beta"][0] + GDN_HEADS
    return pl.pallas_call(
        _pack_small_kernel,
        out_shape=jax.ShapeDtypeStruct((LANES, k), BF16),
        grid=(1,),
        in_specs=[pl.BlockSpec((None, pl.Element(n_gdn), pl.Element(k)), lambda i: (layer, seg["beta"][0], 0)),
                  pl.BlockSpec((None, pl.Element(M2_HEADS), pl.Element(k)), lambda i: (layer, seg["dt"][0], 0))],
        out_specs=pl.BlockSpec((LANES, k), lambda i: (0, 0)),
        name="packs",
    )(wt_all, wt_all)


def _layer(x2d, h, bsz, seqlen, w_in_all, layer, gdn_conv_w, gdn_a_log, gdn_dt_bias, gdn_norm_w,
           s5_lam_re, s5_lam_im, s5_log_step, s5_b_re, s5_b_im, s5_c_re, s5_c_im, s5_d,
           s5_glu_w, s5_glu_b, m2_conv_w, m2_conv_b, m2_a_log, m2_dt_bias, m2_d, m2_norm_w,
           proj_a, proj_b, proj_c, w_out, next_norm_w, final):
    t = bsz * seqlen
    nc = seqlen // CHUNK
    proj = _inproj(h, w_in_all, layer)
    small = _matmul(h, _pack_small(w_in_all, layer), INPROJ_TM, LANES, rhs_transposed=True)
    proj3 = proj.reshape(bsz, seqlen, MAIN_W)
    small3 = small.reshape(bsz, seqlen, LANES)

    ya = _gdn(proj3, small3, gdn_conv_w, gdn_a_log, gdn_dt_bias, gdn_norm_w)
    yc = _ssd(proj3, small3, m2_conv_w, m2_conv_b, m2_a_log, m2_dt_bias, m2_d, m2_norm_w)

    u = proj[:, OFF_SU:OFF_SU + S5_WIDTH].astype(BF16).reshape(bsz * nc, CHUNK, S5_WIDTH)
    u = jnp.swapaxes(u, 1, 2).reshape(bsz * nc, S5_WIDTH * CHUNK)
    ys = _s5(u, s5_lam_re, s5_lam_im, s5_log_step, s5_b_re, s5_b_im, s5_c_re, s5_c_im, nc)
    ys = jnp.swapaxes(ys.reshape(bsz * nc, S5_WIDTH, CHUNK), 1, 2).reshape(t, S5_WIDTH)

    return _merge(x2d, proj, ya.reshape(t, GDN_WIDTH), ys, yc.reshape(t, M2_WIDTH), s5_d,
                  s5_glu_w, s5_glu_b, proj_a, proj_b, proj_c, w_out, next_norm_w, final)


def kernel(x, norm_w, w_in, gdn_conv_w, gdn_a_log, gdn_dt_bias, gdn_norm_w, s5_lam_re, s5_lam_im, s5_log_step, s5_b_re, s5_b_im, s5_c_re, s5_c_im, s5_d, s5_glu_w, s5_glu_b, m2_conv_w, m2_conv_b, m2_a_log, m2_dt_bias, m2_d, m2_norm_w, proj_a, proj_b, proj_c, w_out, final_norm_w):
    bsz, seqlen, d = x.shape
    depth = norm_w.shape[0]
    x2d = x.reshape(bsz * seqlen, d)
    h = _rmsnorm(x2d, norm_w[0], BF16)
    w_in = jnp.swapaxes(w_in, 1, 2)
    for i in range(depth):
        final = i == depth - 1
        res = _layer(x2d, h, bsz, seqlen, w_in, i, gdn_conv_w[i], gdn_a_log[i], gdn_dt_bias[i],
                     gdn_norm_w[i], s5_lam_re[i], s5_lam_im[i], s5_log_step[i], s5_b_re[i], s5_b_im[i],
                     s5_c_re[i], s5_c_im[i], s5_d[i], s5_glu_w[i], s5_glu_b[i],
                     m2_conv_w[i], m2_conv_b[i], m2_a_log[i], m2_dt_bias[i], m2_d[i], m2_norm_w[i],
                     proj_a[i], proj_b[i], proj_c[i], w_out[i],
                     final_norm_w if final else norm_w[i + 1], final)
        if final:
            return res.reshape(bsz, seqlen, d)
        x2d, h = res
```

```python
import functools
import math

import jax
import jax.numpy as jnp
from jax import lax
from jax.experimental import pallas as pl
from jax.experimental.pallas import tpu as pltpu

F32 = jnp.float32
BF16 = jnp.bfloat16
HIGHEST = lax.Precision.HIGHEST

NORM_EPS = 1e-6
CHUNK = 64
CONV_K = 4
HALO = 8
LANES = 128

GDN_HEADS = 8
GDN_HEAD_DIM = 128
GDN_WIDTH = GDN_HEADS * GDN_HEAD_DIM
S5_GROUP_SIZE = 16
S5_GROUPS = 48
S5_STATE = 64
S5_WIDTH = S5_GROUPS * S5_GROUP_SIZE
M2_HEADS = 16
M2_HEAD_DIM = 64
M2_WIDTH = M2_HEADS * M2_HEAD_DIM
M2_GROUPS = 4
M2_STATE = 128
M2_BC = M2_GROUPS * M2_STATE
M2_GROUP_W = M2_WIDTH // M2_GROUPS

OFF_MERGE = 0
OFF_QKV = OFF_MERGE + 3 * 2048
OFF_XBC = OFF_QKV + 3 * GDN_WIDTH
OFF_AZ = OFF_XBC + M2_WIDTH + 2 * M2_BC
OFF_CZ = OFF_AZ + GDN_WIDTH
OFF_SU = OFF_CZ + M2_WIDTH
OFF_SG = OFF_SU + S5_WIDTH
MAIN_W = OFF_SG + S5_WIDTH
LANE_BETA = 0
LANE_DECAY = GDN_HEADS
LANE_DT = 2 * GDN_HEADS

VMEM_LIMIT = 56 * 1024 * 1024
INPROJ_TM = 2048
INPROJ_TN = 512
GDN_CHUNKS_PER_STEP = 1
SSD_CHUNKS_PER_STEP = 2


def _softplus(x):
    return jnp.maximum(x, 0.0) + jnp.log(1.0 + jnp.exp(-jnp.abs(x)))


def _sigmoid(x):
    return 0.5 * jnp.tanh(0.5 * x) + 0.5


def _silu(x):
    h = 0.5 * x
    return h + h * jnp.tanh(h)


def _dot(a, b):
    return jnp.dot(a, b, preferred_element_type=F32)


def _dot_nt(a, b):
    return lax.dot_general(a, b, (((1,), (1,)), ((), ())), preferred_element_type=F32)


def _dot_hi(a, b):
    return jnp.dot(a, b, preferred_element_type=F32, precision=HIGHEST)


def _dot_b(a, b):
    return _dot(a.astype(BF16), b.astype(BF16))


def _split_bf16(x, terms):
    parts = []
    for _ in range(terms):
        p = x.astype(BF16)
        parts.append(p)
        x = x - p.astype(F32)
    return parts


def _cumsum_rows(ltri_b, x):
    n = x.shape[1]
    s = _dot(ltri_b, jnp.concatenate(_split_bf16(x, 3), axis=1))
    return s[:, :n] + s[:, n:2 * n] + s[:, 2 * n:]


def _tri(n, strict=False):
    r = lax.broadcasted_iota(jnp.int32, (n, n), 0)
    c = lax.broadcasted_iota(jnp.int32, (n, n), 1)
    return (r > c) if strict else (r >= c)


def _rmsnorm_kernel(x_ref, w_ref, o_ref):
    x = x_ref[...]
    y = x * lax.rsqrt(jnp.mean(x * x, axis=-1, keepdims=True) + NORM_EPS)
    o_ref[...] = (y * w_ref[...]).astype(o_ref.dtype)


def _rmsnorm(x2d, w, out_dtype, tm=512):
    t, d = x2d.shape
    tm = min(tm, t)
    return pl.pallas_call(
        _rmsnorm_kernel,
        out_shape=jax.ShapeDtypeStruct((t, d), out_dtype),
        grid=(t // tm,),
        in_specs=[pl.BlockSpec((tm, d), lambda i: (i, 0)),
                  pl.BlockSpec((1, d), lambda i: (0, 0))],
        out_specs=pl.BlockSpec((tm, d), lambda i: (i, 0)),
        compiler_params=pltpu.CompilerParams(dimension_semantics=("parallel",)),
        name="rmsnorm",
    )(x2d, w.reshape(1, d))


def _matmul_kernel(a_ref, b_ref, o_ref, *, rhs_transposed):
    dot = _dot_nt if rhs_transposed else _dot
    o_ref[...] = dot(a_ref[...], b_ref[...]).astype(o_ref.dtype)


def _matmul(a, b, tm, tn, out_dtype=F32, rhs_transposed=False):
    m, k = a.shape
    n = b.shape[0] if rhs_transposed else b.shape[1]
    tm = min(tm, m)
    tn = min(tn, n)
    b_spec = (pl.BlockSpec((tn, k), lambda i, j: (j, 0)) if rhs_transposed
              else pl.BlockSpec((k, tn), lambda i, j: (0, j)))
    return pl.pallas_call(
        functools.partial(_matmul_kernel, rhs_transposed=rhs_transposed),
        out_shape=jax.ShapeDtypeStruct((m, n), out_dtype),
        grid=(m // tm, n // tn),
        in_specs=[pl.BlockSpec((tm, k), lambda i, j: (i, 0)), b_spec],
        out_specs=pl.BlockSpec((tm, tn), lambda i, j: (i, j)),
        compiler_params=pltpu.CompilerParams(
            dimension_semantics=("parallel", "parallel"), vmem_limit_bytes=VMEM_LIMIT),
        name="inproj",
    )(a, b)


def _fill_ext(ext_ref, prev_ref, cur_ref, first):
    prev = prev_ref[...]
    ext_ref[0:HALO, :] = jnp.where(first, jnp.zeros_like(prev), prev)
    ext_ref[HALO:, :] = cur_ref[...]


def _conv_tile(ext_ref, w_ref, col, width, row0=0):
    acc = None
    for k in range(CONV_K):
        start = row0 + HALO - (CONV_K - 1) + k
        term = w_ref[k:k + 1, col:col + width] * ext_ref[start:start + CHUNK, col:col + width]
        acc = term if acc is None else acc + term
    return acc


def _gdn_kernel(q_ref, qprev_ref, z_ref, sm_ref, convw_ref, alog_ref, dtb_ref, normw_ref,
                o_ref, state_ref, ext_ref):
    c = pl.program_id(0)
    first = c == 0
    nb = q_ref.shape[0]
    nsub = q_ref.shape[1] // CHUNK
    dk = GDN_HEAD_DIM

    @pl.when(first)
    def _():
        state_ref[...] = jnp.zeros_like(state_ref)

    causal = _tri(CHUNK)
    strict = _tri(CHUNK, strict=True)
    ltri = causal.astype(BF16)
    eye = (lax.broadcasted_iota(jnp.int32, (CHUNK, CHUNK), 0)
           == lax.broadcasted_iota(jnp.int32, (CHUNK, CHUNK), 1)).astype(F32)
    scale = dk ** -0.5

    for b in range(nb):
        _fill_ext(ext_ref.at[b], qprev_ref.at[b], q_ref.at[b], first)

    def per_seq(b, j):
        sm = sm_ref[b, j * CHUNK:(j + 1) * CHUNK, :]
        g_all = -jnp.exp(alog_ref[...]) * _softplus(sm + dtb_ref[...])
        gc_all = _cumsum_rows(ltri, g_all)
        glast_all = gc_all[CHUNK - 1:CHUNK, :]
        return dict(beta=_sigmoid(sm), gc=gc_all, gc_t=gc_all.T, eg=jnp.exp(gc_all),
                    etail=jnp.exp(glast_all - gc_all), eglast=jnp.exp(glast_all))

    seqs = {(b, j): per_seq(b, j) for b in range(nb) for j in range(nsub)}

    items = [(j, b, h) for j in range(nsub) for b in range(nb) for h in range(GDN_HEADS)]
    idx = range(len(items))
    n_chain = nb * GDN_HEADS

    def prep(j, b, h):
        lo = h * dk
        sq = seqs[b, j]
        ext_b = ext_ref.at[b]
        q = _silu(_conv_tile(ext_b, convw_ref, lo, dk, j * CHUNK))
        k = _silu(_conv_tile(ext_b, convw_ref, GDN_WIDTH + lo, dk, j * CHUNK))
        v = _silu(_conv_tile(ext_b, convw_ref, 2 * GDN_WIDTH + lo, dk, j * CHUNK))
        q = q * lax.rsqrt(jnp.sum(q * q, axis=-1, keepdims=True) + NORM_EPS) * scale
        k = k * lax.rsqrt(jnp.sum(k * k, axis=-1, keepdims=True) + NORM_EPS)
        beta = sq["beta"][:, LANE_BETA + h:LANE_BETA + h + 1]
        ld = LANE_DECAY + h
        eg = sq["eg"][:, ld:ld + 1]
        decay = jnp.exp(jnp.where(causal, sq["gc"][:, ld:ld + 1] - sq["gc_t"][ld:ld + 1, :], -jnp.inf))
        kb = k * beta
        return dict(
            lhs_kk=jnp.concatenate([kb, q], axis=0).astype(BF16), k_b=k.astype(BF16), decay=decay,
            rhs_uw=jnp.concatenate([v * beta, kb * eg], axis=1).astype(BF16),
            q_dec=(q * eg).astype(BF16),
            k_tail_t=(k * sq["etail"][:, ld:ld + 1]).T.astype(BF16),
            eglast=sq["eglast"][:, ld:ld + 1])

    hd = [prep(j, b, h) for j, b, h in items]
    kk = [_dot_nt(d["lhs_kk"], d["k_b"]) for d in hd]
    qk = [(kk[i][CHUNK:] * hd[i]["decay"]).astype(BF16) for i in idx]

    b_pow = [-jnp.where(strict, kk[i][:CHUNK] * hd[i]["decay"], 0.0) for i in idx]
    x_inv = [eye + b for b in b_pow]
    b_pow = [_dot_b(b, b) for b in b_pow]
    n_sq = int(math.log2(CHUNK)) - 1
    for it in range(n_sq):
        if it < n_sq - 1:
            p = [_dot_b(jnp.concatenate([b_pow[i], x_inv[i]], axis=0), b_pow[i]) for i in idx]
            b_pow = [p[i][:CHUNK] for i in idx]
            x_inv = [x_inv[i] + p[i][CHUNK:] for i in idx]
        else:
            x_inv = [x_inv[i] + _dot_b(x_inv[i], b_pow[i]) for i in idx]

    uw = [_dot(x_inv[i].astype(BF16), hd[i]["rhs_uw"]) for i in idx]
    state = [state_ref[n] for n in range(n_chain)]
    out = [None] * len(items)
    for j in range(nsub):
        sub = range(j * n_chain, (j + 1) * n_chain)
        ws_qs = {i: _dot(jnp.concatenate([uw[i][:, dk:].astype(BF16), hd[i]["q_dec"]], axis=0),
                         state[i - j * n_chain].astype(BF16)) for i in sub}
        v_new = {i: (uw[i][:, :dk] - ws_qs[i][:CHUNK]).astype(BF16) for i in sub}
        for i in sub:
            out[i] = ws_qs[i][CHUNK:] + _dot(qk[i], v_new[i])
        state = [state[i - j * n_chain] * hd[i]["eglast"] + _dot(hd[i]["k_tail_t"], v_new[i]) for i in sub]
    for n in range(n_chain):
        state_ref[n] = state[n]
    for i, (j, b, h) in enumerate(items):
        lo = h * dk
        rows = slice(j * CHUNK, (j + 1) * CHUNK)
        o = out[i] * lax.rsqrt(jnp.mean(out[i] * out[i], axis=-1, keepdims=True) + NORM_EPS) * normw_ref[...]
        o_ref[b, rows, lo:lo + dk] = (o * _silu(z_ref[b, rows, lo:lo + dk])).astype(o_ref.dtype)


def _lane_vec(values, lane0):
    v = jnp.zeros((1, LANES), F32)
    return lax.dynamic_update_slice(v, values.astype(F32).reshape(1, -1), (0, lane0))


def _gdn(proj, small, conv_w, a_log, dt_bias, norm_w):
    bsz, seqlen, _ = proj.shape
    nc = seqlen // CHUNK
    wq = 3 * GDN_WIDTH
    nsub = GDN_CHUNKS_PER_STEP if nc % GDN_CHUNKS_PER_STEP == 0 else 1
    tb = nsub * CHUNK
    rows_per_halo = tb // HALO
    return pl.pallas_call(
        _gdn_kernel,
        out_shape=jax.ShapeDtypeStruct((bsz, seqlen, GDN_WIDTH), F32),
        grid=(nc // nsub,),
        in_specs=[
            pl.BlockSpec((bsz, tb, wq), lambda c: (0, c, OFF_QKV // wq)),
            pl.BlockSpec((bsz, HALO, wq),
                         lambda c: (0, jnp.maximum(c * rows_per_halo - 1, 0), OFF_QKV // wq)),
            pl.BlockSpec((bsz, tb, GDN_WIDTH), lambda c: (0, c, OFF_AZ // GDN_WIDTH)),
            pl.BlockSpec((bsz, tb, LANES), lambda c: (0, c, 0)),
            pl.BlockSpec((CONV_K, wq), lambda c: (0, 0)),
            pl.BlockSpec((1, LANES), lambda c: (0, 0)),
            pl.BlockSpec((1, LANES), lambda c: (0, 0)),
            pl.BlockSpec((1, GDN_HEAD_DIM), lambda c: (0, 0)),
        ],
        out_specs=pl.BlockSpec((bsz, tb, GDN_WIDTH), lambda c: (0, c, 0)),
        scratch_shapes=[pltpu.VMEM((bsz * GDN_HEADS, GDN_HEAD_DIM, GDN_HEAD_DIM), F32),
                        pltpu.VMEM((bsz, HALO + tb, wq), F32)],
        compiler_params=pltpu.CompilerParams(
            dimension_semantics=("arbitrary",), vmem_limit_bytes=VMEM_LIMIT),
        name="gdn",
    )(proj, proj, proj, small, conv_w.astype(F32), _lane_vec(a_log, LANE_DECAY),
      _lane_vec(dt_bias, LANE_DECAY), norm_w.astype(F32).reshape(1, GDN_HEAD_DIM))


def _ssd_kernel(x_ref, bc_ref, xprev_ref, bcprev_ref, z_ref, sm_ref, convw_ref, convb_ref,
                alog_ref, dtb_ref, dskip_ref, normw_ref, expand_ref,
                o_ref, state_ref, ext_ref):
    c = pl.program_id(0)
    first = c == 0
    nb = x_ref.shape[0]

    @pl.when(first)
    def _():
        state_ref[...] = jnp.zeros_like(state_ref)

    causal = _tri(CHUNK)
    ltri = causal.astype(BF16)
    expand = expand_ref[...]

    nsub = x_ref.shape[1] // CHUNK
    for b in range(nb):
        prev_x = xprev_ref[b]
        prev_bc = bcprev_ref[b]
        ext_ref[b, 0:HALO, 0:M2_WIDTH] = jnp.where(first, jnp.zeros_like(prev_x), prev_x)
        ext_ref[b, 0:HALO, M2_WIDTH:] = jnp.where(first, jnp.zeros_like(prev_bc), prev_bc)
        ext_ref[b, HALO:, 0:M2_WIDTH] = x_ref[b]
        ext_ref[b, HALO:, M2_WIDTH:] = bc_ref[b]

    def per_seq(b, j):
        dt_all = _softplus(sm_ref[b, j * CHUNK:(j + 1) * CHUNK, :] + dtb_ref[...])
        a_all = -jnp.exp(alog_ref[...]) * dt_all
        acum = _cumsum_rows(ltri, a_all)
        ea = jnp.exp(acum)
        ds = jnp.exp(acum[CHUNK - 1:CHUNK, :] - acum)
        sp_hi, sp_lo = _split_bf16(jnp.concatenate([dt_all, ea, ds], axis=0), 2)
        spread = _dot(jnp.concatenate([sp_hi, sp_lo], axis=0), expand)
        spread = spread[:3 * CHUNK] + spread[3 * CHUNK:]
        ea_x = spread[CHUNK:2 * CHUNK]
        return dict(acum=acum, acum_t=acum.T, dt_x=spread[:CHUNK], ea_x=ea_x, ds_x=spread[2 * CHUNK:],
                    cdecay_x=ea_x[CHUNK - 1:CHUNK, :])

    seqs = {(b, j): per_seq(b, j) for b in range(nb) for j in range(nsub)}

    lane = lax.broadcasted_iota(jnp.int32, (CHUNK, 2 * M2_HEAD_DIM), 1)
    heads_per_group = M2_HEADS // M2_GROUPS

    items = [(j, b, g) for j in range(nsub) for b in range(nb) for g in range(M2_GROUPS)]
    idx = range(len(items))
    n_chain = nb * M2_GROUPS

    def prep(j, b, g):
        glo = g * M2_GROUP_W
        bcol = M2_WIDTH + g * M2_STATE
        ccol = M2_WIDTH + M2_BC + g * M2_STATE
        ext_b = ext_ref.at[b]
        r0 = j * CHUNK
        bm = _silu(_conv_tile(ext_b, convw_ref, bcol, M2_STATE, r0) + convb_ref[:, bcol:bcol + M2_STATE])
        cm = _silu(_conv_tile(ext_b, convw_ref, ccol, M2_STATE, r0) + convb_ref[:, ccol:ccol + M2_STATE])
        xs = _silu(_conv_tile(ext_b, convw_ref, glo, M2_GROUP_W, r0) + convb_ref[:, glo:glo + M2_GROUP_W])
        xdt = xs * seqs[b, j]["dt_x"][:, glo:glo + M2_GROUP_W]
        return dict(xs=xs, xdt_b=xdt.astype(BF16),
                    xds=(xdt * seqs[b, j]["ds_x"][:, glo:glo + M2_GROUP_W]).astype(BF16),
                    bm_b=bm.astype(BF16), bm_t=bm.T.astype(BF16), cm_b=cm.astype(BF16))

    gd = [prep(j, b, g) for j, b, g in items]
    scores = [_dot_nt(d["cm_b"], d["bm_b"]) for d in gd]
    states = [_dot(d["bm_t"], d["xds"]) for d in gd]
    state = [state_ref[n] for n in range(n_chain)]
    s_in = [None] * len(items)
    for i, (j, b, g) in enumerate(items):
        n = i - j * n_chain
        glo = g * M2_GROUP_W
        s_in[i] = state[n]
        state[n] = state[n] * seqs[b, j]["cdecay_x"][:, glo:glo + M2_GROUP_W] + states[i]
    for n in range(n_chain):
        state_ref[n] = state[n]
    y_off = [_dot(gd[i]["cm_b"], s_in[i].astype(BF16)) for i in idx]

    def diag(i, jp, r):
        j, b, g = items[i]
        la = LANE_DT + g * heads_per_group + 2 * jp + r
        sq = seqs[b, j]
        seg = jnp.exp(jnp.where(causal, sq["acum"][:, la:la + 1] - sq["acum_t"][la:la + 1, :], -jnp.inf))
        x2 = gd[i]["xdt_b"][:, jp * 2 * M2_HEAD_DIM:(jp + 1) * 2 * M2_HEAD_DIM]
        return _dot((scores[i] * seg).astype(BF16), x2)

    pairs = range(heads_per_group // 2)
    yd = [[[diag(i, jp, r) for r in range(2)] for jp in pairs] for i in idx]
    for i, (j, b, g) in enumerate(items):
        glo = g * M2_GROUP_W
        rows = slice(j * CHUNK, (j + 1) * CHUNK)
        parts = [jnp.where(lane < M2_HEAD_DIM, yd[i][jp][0], yd[i][jp][1]) for jp in pairs]
        y = jnp.concatenate(parts, axis=1) + y_off[i] * seqs[b, j]["ea_x"][:, glo:glo + M2_GROUP_W]
        y = y + dskip_ref[:, glo:glo + M2_GROUP_W] * gd[i]["xs"]
        y = y * _silu(z_ref[b, rows, glo:glo + M2_GROUP_W])
        y = y * lax.rsqrt(jnp.mean(y * y, axis=-1, keepdims=True) + NORM_EPS)
        o_ref[b, rows, glo:glo + M2_GROUP_W] = (y * normw_ref[:, glo:glo + M2_GROUP_W]).astype(o_ref.dtype)


def _ssd(proj, small, conv_w, conv_b, a_log, dt_bias, d_skip, norm_w):
    bsz, seqlen, _ = proj.shape
    nc = seqlen // CHUNK
    nsub = SSD_CHUNKS_PER_STEP if nc % SSD_CHUNKS_PER_STEP == 0 else 1
    tb = nsub * CHUNK
    rows_per_halo = tb // HALO
    conv_dim = M2_WIDTH + 2 * M2_BC
    head_of_col = jnp.arange(M2_WIDTH) // M2_HEAD_DIM
    expand = (jnp.arange(LANES)[:, None] == (LANE_DT + head_of_col)[None, :]).astype(BF16)
    dskip_x = jnp.repeat(d_skip.astype(F32), M2_HEAD_DIM).reshape(1, M2_WIDTH)
    xi = OFF_XBC // M2_WIDTH

    def halo_idx(c):
        return jnp.maximum(c * rows_per_halo - 1, 0)

    return pl.pallas_call(
        _ssd_kernel,
        out_shape=jax.ShapeDtypeStruct((bsz, seqlen, M2_WIDTH), F32),
        grid=(nc // nsub,),
        in_specs=[
            pl.BlockSpec((bsz, tb, M2_WIDTH), lambda c: (0, c, xi)),
            pl.BlockSpec((bsz, tb, 2 * M2_BC), lambda c: (0, c, xi + 1)),
            pl.BlockSpec((bsz, HALO, M2_WIDTH), lambda c: (0, halo_idx(c), xi)),
            pl.BlockSpec((bsz, HALO, 2 * M2_BC), lambda c: (0, halo_idx(c), xi + 1)),
            pl.BlockSpec((bsz, tb, M2_WIDTH), lambda c: (0, c, OFF_CZ // M2_WIDTH)),
            pl.BlockSpec((bsz, tb, LANES), lambda c: (0, c, 0)),
            pl.BlockSpec((CONV_K, conv_dim), lambda c: (0, 0)),
            pl.BlockSpec((1, conv_dim), lambda c: (0, 0)),
            pl.BlockSpec((1, LANES), lambda c: (0, 0)),
            pl.BlockSpec((1, LANES), lambda c: (0, 0)),
            pl.BlockSpec((1, M2_WIDTH), lambda c: (0, 0)),
            pl.BlockSpec((1, M2_WIDTH), lambda c: (0, 0)),
            pl.BlockSpec((LANES, M2_WIDTH), lambda c: (0, 0)),
        ],
        out_specs=pl.BlockSpec((bsz, tb, M2_WIDTH), lambda c: (0, c, 0)),
        scratch_shapes=[pltpu.VMEM((bsz * M2_GROUPS, M2_STATE, M2_GROUP_W), F32),
                        pltpu.VMEM((bsz, HALO + tb, conv_dim), F32)],
        compiler_params=pltpu.CompilerParams(
            dimension_semantics=("arbitrary",), vmem_limit_bytes=VMEM_LIMIT),
        name="ssd",
    )(proj, proj, proj, proj, proj, small, conv_w.astype(F32), conv_b.astype(F32).reshape(1, conv_dim),
      _lane_vec(a_log, LANE_DT), _lane_vec(dt_bias, LANE_DT), dskip_x,
      norm_w.astype(F32).reshape(1, M2_WIDTH), expand)


def _s5_kernel(u_ref, lamre_c_ref, lamim_c_ref, lamre_r_ref, lamim_r_ref, lstep_ref,
               bre_c_ref, bim_c_ref, bre_r_ref, bim_r_ref, cre_c_ref, cim_c_ref, rexp_ref,
               o_ref, toep_ref, *, chunks_per_seq):
    p = S5_STATE
    hs = S5_GROUP_SIZE
    cw = CHUNK * hs
    step = jnp.exp(lstep_ref[0:1, :])
    step_r = step[:, :p]

    def disc(lre, lim, st):
        lre = jnp.minimum(lre, -1e-4)
        mag = jnp.exp(lre * st)
        are = mag * jnp.cos(lim * st)
        aim = mag * jnp.sin(lim * st)
        den = lre * lre + lim * lim
        fre = ((are - 1.0) * lre + aim * lim) / den
        fim = (aim * lre - (are - 1.0) * lim) / den
        return lre, are, aim, fre, fim

    lre_c, are_c, aim_c, fre_c, fim_c = disc(lamre_c_ref[...], lamim_c_ref[...], step)
    lim_c = lamim_c_ref[...]
    lre_r, _, _, fre_r, fim_r = disc(lamre_r_ref[0:1, :], lamim_r_ref[0:1, :], step_r)
    lim_r = lamim_r_ref[0:1, :]
    bbre_r = fre_r * bre_r_ref[...] - fim_r * bim_r_ref[...]
    bbim_r = fre_r * bim_r_ref[...] + fim_r * bre_r_ref[...]

    assert LANES % CHUNK == 0
    ntile = cw // LANES
    t_lane = (lax.broadcasted_iota(jnp.int32, (p, LANES), 1) & (CHUNK - 1)).astype(F32)

    def cmul(a, b):
        return a[0] * b[0] - a[1] * b[1], a[0] * b[1] + a[1] * b[0]

    def cpow(expo):
        mag = jnp.exp(lre_c * step * expo)
        ang = lim_c * step * expo
        return mag * jnp.cos(ang), mag * jnp.sin(ang)

    def tiled(z):
        return (jnp.concatenate([z[0]] * ntile, axis=1), jnp.concatenate([z[1]] * ntile, axis=1))

    stack = jnp.concatenate([bre_c_ref[...], bim_c_ref[...], cre_c_ref[...], cim_c_ref[...]], axis=0)
    spread = _dot(jnp.concatenate(_split_bf16(stack, 3), axis=0), rexp_ref[...])
    spread = spread[:4 * p] + spread[4 * p:8 * p] + spread[8 * p:]
    bexp = (spread[:p], spread[p:2 * p])
    cexp = (spread[2 * p:3 * p], spread[3 * p:])

    m0re, m0im = cmul(cexp, tiled(cpow(t_lane)))
    m1re, m1im = cmul((m0re, m0im), tiled((are_c, aim_c)))
    bb = cmul(tiled((fre_c, fim_c)), bexp)
    nre, nim = cmul(bb, tiled(cpow(float(CHUNK - 1) - t_lane)))

    kt = _dot_hi(bbre_r, m0re) - _dot_hi(bbim_r, m0im)

    causal_lane = ((lax.broadcasted_iota(jnp.int32, (CHUNK, cw), 1) & (CHUNK - 1))
                   >= lax.broadcasted_iota(jnp.int32, (CHUNK, cw), 0))
    for hi in range(hs):
        rows_hi = jnp.broadcast_to(kt[hi:hi + 1, :], (CHUNK, cw))
        shifted = pltpu.roll(rows_hi, 0, 1, stride=1, stride_axis=0)
        toep_ref[hi * CHUNK:(hi + 1) * CHUNK, :] = jnp.where(causal_lane, shifted, 0.0).astype(BF16)

    ub = u_ref[...]
    y = _dot(ub, toep_ref[...])
    sre = _dot_nt(ub, nre.astype(BF16))
    sim = _dot_nt(ub, nim.astype(BF16))

    rows = ub.shape[0]
    cidx = lax.broadcasted_iota(jnp.int32, (rows, p), 0) & (chunks_per_seq - 1)
    mag_c = jnp.exp(lre_r * step_r * float(CHUNK))
    pre = mag_c * jnp.cos(lim_r * step_r * float(CHUNK))
    pim = mag_c * jnp.sin(lim_r * step_r * float(CHUNK))
    d = 1
    while d < chunks_per_seq:
        ok = cidx >= d
        shre = jnp.where(ok, pltpu.roll(sre, d, 0), 0.0)
        shim = jnp.where(ok, pltpu.roll(sim, d, 0), 0.0)
        sre, sim = sre + pre * shre - pim * shim, sim + pre * shim + pim * shre
        pre, pim = pre * pre - pim * pim, 2.0 * pre * pim
        d *= 2
    ok = cidx >= 1
    xsre = jnp.where(ok, pltpu.roll(sre, 1, 0), 0.0)
    xsim = jnp.where(ok, pltpu.roll(sim, 1, 0), 0.0)
    y = y + _dot(xsre.astype(BF16), m1re.astype(BF16)) - _dot(xsim.astype(BF16), m1im.astype(BF16))
    o_ref[...] = y.astype(o_ref.dtype)


def _s5(u3, lam_re, lam_im, log_step, b_re, b_im, c_re, c_im, chunks_per_seq):
    g = lam_re.shape[0]
    rows = u3.shape[0]
    cw = u3.shape[1] // g
    p, hs = S5_STATE, S5_GROUP_SIZE
    f32 = lambda a: a.astype(F32)
    col = lambda a: jnp.broadcast_to(f32(a)[:, :, None], (g, p, LANES))
    row = lambda a: jnp.broadcast_to(f32(a)[:, None, :], (g, HALO, p))
    lstep = jnp.broadcast_to(f32(log_step)[:, None, None], (g, HALO, LANES))
    b_c = f32
    b_r = lambda a: jnp.swapaxes(f32(a), 1, 2)
    c_c = lambda a: jnp.swapaxes(f32(a), 1, 2)
    rexp = (jnp.arange(hs)[:, None] == (jnp.arange(cw) // CHUNK)[None, :]).astype(BF16)
    io_spec = pl.BlockSpec((rows, cw), lambda i: (0, i))

    def gspec(shape):
        return pl.BlockSpec((None,) + shape, lambda i: (i, 0, 0))

    return pl.pallas_call(
        functools.partial(_s5_kernel, chunks_per_seq=chunks_per_seq),
        out_shape=jax.ShapeDtypeStruct((rows, g * cw), BF16),
        grid=(g,),
        in_specs=[io_spec,
                  gspec((p, LANES)), gspec((p, LANES)), gspec((HALO, p)), gspec((HALO, p)),
                  gspec((HALO, LANES)),
                  gspec((p, hs)), gspec((p, hs)), gspec((hs, p)), gspec((hs, p)),
                  gspec((p, hs)), gspec((p, hs)),
                  pl.BlockSpec((hs, cw), lambda i: (0, 0))],
        out_specs=io_spec,
        scratch_shapes=[pltpu.VMEM((cw, cw), BF16)],
        compiler_params=pltpu.CompilerParams(
            dimension_semantics=("parallel",), vmem_limit_bytes=VMEM_LIMIT),
        name="s5",
    )(u3, col(lam_re), col(lam_im), row(lam_re), row(lam_im), lstep,
      b_c(b_re), b_c(b_im), b_r(b_re), b_r(b_im), c_c(c_re), c_c(c_im), rexp)


def _merge_kernel(x_ref, gate_ref, ya_ref, ys_ref, su0_ref, su1_ref, su2_ref, sg0_ref, sg1_ref, sg2_ref, yc_ref,
                  sd_ref, gluw_ref, glub_ref, pa_ref, pb_ref, pc_ref, wout_ref, nw_ref, o_ref, *h_ref, final):
    d = x_ref.shape[-1]
    su = jnp.concatenate([su0_ref[...], su1_ref[...], su2_ref[...]], axis=1)
    ys = jax.nn.gelu(ys_ref[...].astype(F32) + sd_ref[...] * su)
    glu = _sigmoid(_dot(ys.astype(BF16), gluw_ref[...]) + glub_ref[...])
    sgate = jnp.concatenate([sg0_ref[...], sg1_ref[...], sg2_ref[...]], axis=1)
    yb = ys * glu * _silu(sgate)
    pa = _dot(ya_ref[...].astype(BF16), pa_ref[...])
    pb = _dot(yb.astype(BF16), pb_ref[...])
    pc = _dot(yc_ref[...].astype(BF16), pc_ref[...])
    merged = (_sigmoid(gate_ref[:, 0:d]) * pa
              + _sigmoid(gate_ref[:, d:2 * d]) * pb
              + _sigmoid(gate_ref[:, 2 * d:3 * d]) * pc)
    out = x_ref[...] + _dot(merged.astype(BF16), wout_ref[...])
    normed = out * lax.rsqrt(jnp.mean(out * out, axis=-1, keepdims=True) + NORM_EPS) * nw_ref[...]
    if final:
        o_ref[...] = normed.astype(o_ref.dtype)
    else:
        o_ref[...] = out.astype(o_ref.dtype)
        h_ref[0][...] = normed.astype(h_ref[0].dtype)


def _merge(x2d, proj2d, ya, ys, yc, s5_d, glu_w, glu_b, proj_a, proj_b, proj_c, w_out, next_norm_w, final, tm=256):
    t, d = x2d.shape
    tm = min(tm, t)
    sgw = S5_WIDTH // 3
    assert OFF_SG % sgw == 0 and OFF_SU % sgw == 0 and sgw % LANES == 0

    def wspec(shape):
        return pl.BlockSpec(shape, lambda i: (0, 0), pipeline_mode=pl.Buffered(1))

    row_spec = pl.BlockSpec((tm, d), lambda i: (i, 0))
    if final:
        out_shape, out_specs = jax.ShapeDtypeStruct((t, d), F32), row_spec
    else:
        out_shape = (jax.ShapeDtypeStruct((t, d), F32), jax.ShapeDtypeStruct((t, d), BF16))
        out_specs = (row_spec, row_spec)

    return pl.pallas_call(
        functools.partial(_merge_kernel, final=final),
        out_shape=out_shape,
        grid=(t // tm,),
        in_specs=[
            pl.BlockSpec((tm, d), lambda i: (i, 0)),
            pl.BlockSpec((tm, 3 * d), lambda i: (i, OFF_MERGE // (3 * d))),
            pl.BlockSpec((tm, GDN_WIDTH), lambda i: (i, 0)),
            pl.BlockSpec((tm, S5_WIDTH), lambda i: (i, 0)),
            pl.BlockSpec((tm, sgw), lambda i: (i, OFF_SU // sgw)),
            pl.BlockSpec((tm, sgw), lambda i: (i, OFF_SU // sgw + 1)),
            pl.BlockSpec((tm, sgw), lambda i: (i, OFF_SU // sgw + 2)),
            pl.BlockSpec((tm, sgw), lambda i: (i, OFF_SG // sgw)),
            pl.BlockSpec((tm, sgw), lambda i: (i, OFF_SG // sgw + 1)),
            pl.BlockSpec((tm, sgw), lambda i: (i, OFF_SG // sgw + 2)),
            pl.BlockSpec((tm, M2_WIDTH), lambda i: (i, 0)),
            wspec((1, S5_WIDTH)),
            wspec((S5_WIDTH, S5_WIDTH)), wspec((1, S5_WIDTH)),
            wspec((GDN_WIDTH, d)), wspec((S5_WIDTH, d)), wspec((M2_WIDTH, d)), wspec((d, d)),
            wspec((1, d)),
        ],
        out_specs=out_specs,
        compiler_params=pltpu.CompilerParams(
            dimension_semantics=("parallel",), vmem_limit_bytes=VMEM_LIMIT),
        name="merge",
    )(x2d, proj2d, ya, ys, proj2d, proj2d, proj2d, proj2d, proj2d, proj2d, yc,
      s5_d.astype(F32).reshape(1, S5_WIDTH), glu_w.astype(BF16), glu_b.astype(F32).reshape(1, -1),
      proj_a.astype(BF16), proj_b.astype(BF16), proj_c.astype(BF16), w_out.astype(BF16),
      next_norm_w.astype(F32).reshape(1, d))


def _w_in_segments(d_model):
    conv_dim = M2_WIDTH + 2 * M2_BC
    o = 0
    seg = {}
    for name, width in (("qkv", 3 * GDN_WIDTH), ("az", GDN_WIDTH), ("beta", GDN_HEADS), ("decay", GDN_HEADS),
                        ("su", S5_WIDTH), ("sg", S5_WIDTH), ("cz", M2_WIDTH), ("xbc", conv_dim),
                        ("dt", M2_HEADS), ("merge", 3 * d_model)):
        seg[name] = (o, width)
        o += width
    return seg


_MAIN_ORDER = (("merge", OFF_MERGE), ("qkv", OFF_QKV), ("xbc", OFF_XBC), ("az", OFF_AZ), ("cz", OFF_CZ),
               ("su", OFF_SU), ("sg", OFF_SG))


def _inproj_kernel(src_ref, h_ref, wt_ref, o_ref):
    del src_ref
    o_ref[...] = _dot_nt(h_ref[...], wt_ref[...].astype(BF16)).astype(o_ref.dtype)


def _inproj(h, wt_all, layer):
    m, k = h.shape
    seg = _w_in_segments(k)
    tm, tn = min(INPROJ_TM, m), INPROJ_TN

    def src_of(col):
        name, dst = [(nm, d0) for nm, d0 in _MAIN_ORDER if d0 <= col < d0 + seg[nm][1]][0]
        return seg[name][0] + col - dst

    src_rows = [src_of(j * tn) for j in range(MAIN_W // tn)]
    assert all(src_of(j * tn + tn - 1) == s + tn - 1 for j, s in enumerate(src_rows))
    return pl.pallas_call(
        _inproj_kernel,
        out_shape=jax.ShapeDtypeStruct((m, MAIN_W), F32),
        grid_spec=pltpu.PrefetchScalarGridSpec(
            num_scalar_prefetch=1,
            grid=(m // tm, MAIN_W // tn),
            in_specs=[pl.BlockSpec((tm, k), lambda i, j, src: (i, 0)),
                      pl.BlockSpec((None, pl.Element(tn), pl.Element(k)),
                                   lambda i, j, src: (layer, pl.multiple_of(src[j], HALO), 0))],
            out_specs=pl.BlockSpec((tm, tn), lambda i, j, src: (i, j))),
        compiler_params=pltpu.CompilerParams(
            dimension_semantics=("parallel", "arbitrary"), vmem_limit_bytes=VMEM_LIMIT),
        name="inproj",
    )(jnp.asarray(src_rows, jnp.int32), h, wt_all)


def _pack_small_kernel(gdn_ref, dt_ref, o_ref):
    o_ref[...] = jnp.zeros_like(o_ref)
    o_ref[LANE_BETA:LANE_BETA + 2 * GDN_HEADS, :] = gdn_ref[...].astype(o_ref.dtype)
    o_ref[LANE_DT:LANE_DT + M2_HEADS, :] = dt_ref[...].astype(o_ref.dtype)


def _pack_small(wt_all, layer):
    _, n, k = wt_all.shape
    seg = _w_in_segments(k)
    n_gdn = 2 * GDN_HEADS
    assert seg["decay"][0] == seg["beta"][0] + GDN_HEADS
    return pl.pallas_call(
        _pack_small_kernel,
        out_shape=jax.ShapeDtypeStruct((LANES, k), BF16),
        grid=(1,),
        in_specs=[pl.BlockSpec((None, pl.Element(n_gdn), pl.Element(k)), lambda i: (layer, seg["beta"][0], 0)),
                  pl.BlockSpec((None, pl.Element(M2_HEADS), pl.Element(k)), lambda i: (layer, seg["dt"][0], 0))],
        out_specs=pl.BlockSpec((LANES, k), lambda i: (0, 0)),
        name="packs",
    )(wt_all, wt_all)


def _layer(x2d, h, bsz, seqlen, w_in_all, layer, gdn_conv_w, gdn_a_log, gdn_dt_bias, gdn_norm_w,
           s5_lam_re, s5_lam_im, s5_log_step, s5_b_re, s5_b_im, s5_c_re, s5_c_im, s5_d,
           s5_glu_w, s5_glu_b, m2_conv_w, m2_conv_b, m2_a_log, m2_dt_bias, m2_d, m2_norm_w,
           proj_a, proj_b, proj_c, w_out, next_norm_w, final):
    t = bsz * seqlen
    nc = seqlen // CHUNK
    proj = _inproj(h, w_in_all, layer)
    small = _matmul(h, _pack_small(w_in_all, layer), INPROJ_TM, LANES, rhs_transposed=True)
    proj3 = proj.reshape(bsz, seqlen, MAIN_W)
    small3 = small.reshape(bsz, seqlen, LANES)

    ya = _gdn(proj3, small3, gdn_conv_w, gdn_a_log, gdn_dt_bias, gdn_norm_w)
    yc = _ssd(proj3, small3, m2_conv_w, m2_conv_b, m2_a_log, m2_dt_bias, m2_d, m2_norm_w)

    u = proj[:, OFF_SU:OFF_SU + S5_WIDTH].astype(BF16).reshape(bsz * nc, CHUNK, S5_WIDTH)
    u = jnp.swapaxes(u, 1, 2).reshape(bsz * nc, S5_WIDTH * CHUNK)
    ys = _s5(u, s5_lam_re, s5_lam_im, s5_log_step, s5_b_re, s5_b_im, s5_c_re, s5_c_im, nc)
    ys = jnp.swapaxes(ys.reshape(bsz * nc, S5_WIDTH, CHUNK), 1, 2).reshape(t, S5_WIDTH)

    return _merge(x2d, proj, ya.reshape(t, GDN_WIDTH), ys, yc.reshape(t, M2_WIDTH), s5_d,
                  s5_glu_w, s5_glu_b, proj_a, proj_b, proj_c, w_out, next_norm_w, final)


def kernel(x, norm_w, w_in, gdn_conv_w, gdn_a_log, gdn_dt_bias, gdn_norm_w, s5_lam_re, s5_lam_im, s5_log_step, s5_b_re, s5_b_im, s5_c_re, s5_c_im, s5_d, s5_glu_w, s5_glu_b, m2_conv_w, m2_conv_b, m2_a_log, m2_dt_bias, m2_d, m2_norm_w, proj_a, proj_b, proj_c, w_out, final_norm_w):
    bsz, seqlen, d = x.shape
    depth = norm_w.shape[0]
    x2d = x.reshape(bsz * seqlen, d)
    h = _rmsnorm(x2d, norm_w[0], BF16)
    w_in = jnp.swapaxes(w_in, 1, 2)
    for i in range(depth):
        final = i == depth - 1
        res = _layer(x2d, h, bsz, seqlen, w_in, i, gdn_conv_w[i], gdn_a_log[i], gdn_dt_bias[i],
                     gdn_norm_w[i], s5_lam_re[i], s5_lam_im[i], s5_log_step[i], s5_b_re[i], s5_b_im[i],
                     s5_c_re[i], s5_c_im[i], s5_d[i], s5_glu_w[i], s5_glu_b[i],
                     m2_conv_w[i], m2_conv_b[i], m2_a_log[i], m2_dt_bias[i], m2_d[i], m2_norm_w[i],
                     proj_a[i], proj_b[i], proj_c[i], w_out[i],
                     final_norm_w if final else norm_w[i + 1], final)
        if final:
            return res.reshape(bsz, seqlen, d)
        x2d, h = res
```
